```python
import jax, jax.numpy as jnp
from jax import lax
import numpy as np

D_MODEL = 1024
BATCH = 8
SEQ = 4096
DEPTH = 2

N_A_LAYERS = DEPTH // 2
N_B_LAYERS = DEPTH - N_A_LAYERS
PLE_DIM = 256
EPS = 1e-6

ML_HEADS = 8
ML_DV = D_MODEL // ML_HEADS
ML_DQK = ML_DV // 2
ML_CHUNK = 64
GATE_CAP = 15.0
ML_SPLITS = (ML_HEADS * ML_DQK, ML_HEADS * ML_DQK, ML_HEADS * ML_DV, ML_HEADS * ML_DV, ML_HEADS, ML_HEADS)
ML_IN = sum(ML_SPLITS)

MB_HEADS = 16
MB_KV_HEADS = 4
MB_HD = D_MODEL // MB_HEADS
MB_GROUP = MB_HEADS // MB_KV_HEADS
MB_BLOCK = 256
MB_TOPK = 3
MB_QCHUNK = 16
ROPE_THETA = 500000.0
ROPE_DIM = MB_HD // 4

FFN_HIDDEN = -(-8 * D_MODEL // (3 * 256)) * 256

kernel_name = "yoco_mlstm_moba_hybrid"


def rmsnorm(x, g):
    xf = x.astype(jnp.float32)
    y = xf * lax.rsqrt(jnp.mean(xf * xf, axis=-1, keepdims=True) + EPS)
    return (y * g.astype(jnp.float32)).astype(x.dtype)


def rope_partial(x):
    s_len = x.shape[2]
    half = ROPE_DIM // 2
    inv = ROPE_THETA ** (-jnp.arange(0, ROPE_DIM, 2, dtype=jnp.float32) / ROPE_DIM)
    ang = jnp.arange(s_len, dtype=jnp.float32)[:, None] * inv[None, :]
    cos, sin = jnp.cos(ang), jnp.sin(ang)
    xf = x.astype(jnp.float32)
    x1, x2 = xf[..., :half], xf[..., half:ROPE_DIM]
    out = jnp.concatenate([x1 * cos - x2 * sin, x2 * cos + x1 * sin, xf[..., ROPE_DIM:]], axis=-1)
    return out.astype(x.dtype)


def swiglu(h, w_gate_up, w_down):
    g, u = jnp.split(h @ w_gate_up, 2, axis=-1)
    return (jax.nn.silu(g) * u) @ w_down


def mlstm_chunkwise(q, k, v, logi, logf):
    b, nh, s_len, dqk = q.shape
    dv = v.shape[-1]
    L = ML_CHUNK
    nc = s_len // L
    qc = q.reshape(b, nh, nc, L, dqk) * (dqk ** -0.5)
    kc = k.reshape(b, nh, nc, L, dqk)
    vc = v.reshape(b, nh, nc, L, dv)
    li = logi.reshape(b, nh, nc, L)
    g = jnp.cumsum(logf.reshape(b, nh, nc, L), axis=-1)
    g_last = g[..., -1]
    causal = jnp.tril(jnp.ones((L, L), dtype=bool))
    dmat = jnp.where(causal, g[..., :, None] - g[..., None, :] + li[..., None, :], -jnp.inf)
    m_intra = jnp.max(dmat, axis=-1)
    a = g_last[..., None] - g + li
    m_loc = jnp.max(a, axis=-1)
    wloc = jnp.exp(a - m_loc[..., None])
    c_loc = jnp.einsum('bhcsv,bhcsk->bhcvk', vc * wloc[..., None], kc)
    n_loc = jnp.einsum('bhcs,bhcsk->bhck', wloc, kc)

    def step(carry, inp):
        c_st, n_st, m_st = carry
        cl, nl, ml, gl = inp
        m_new = jnp.maximum(gl + m_st, ml)
        sp = jnp.exp(gl + m_st - m_new)
        sl = jnp.exp(ml - m_new)
        c_new = sp[..., None, None] * c_st + sl[..., None, None] * cl
        n_new = sp[..., None] * n_st + sl[..., None] * nl
        return (c_new, n_new, m_new), (c_st, n_st, m_st)

    init = (jnp.zeros((b, nh, dv, dqk), jnp.float32), jnp.zeros((b, nh, dqk), jnp.float32),
            jnp.zeros((b, nh), jnp.float32))
    xs = (jnp.moveaxis(c_loc, 2, 0), jnp.moveaxis(n_loc, 2, 0), jnp.moveaxis(m_loc, 2, 0), jnp.moveaxis(g_last, 2, 0))
    _, (c_prev, n_prev, m_prev) = lax.scan(step, init, xs)
    c_prev = jnp.moveaxis(c_prev, 0, 2)
    n_prev = jnp.moveaxis(n_prev, 0, 2)
    m_prev = jnp.moveaxis(m_prev, 0, 2)

    m_inter = g + m_prev[..., None]
    m_comb = jnp.maximum(m_inter, m_intra)
    w_inter = jnp.exp(m_inter - m_comb)
    s = jnp.einsum('bhcjk,bhcsk->bhcjs', qc, kc) * jnp.exp(dmat - m_comb[..., None])
    num = w_inter[..., None] * jnp.einsum('bhcjk,bhcvk->bhcjv', qc, c_prev) + jnp.einsum('bhcjs,bhcsv->bhcjv', s, vc)
    den = w_inter * jnp.einsum('bhcjk,bhck->bhcj', qc, n_prev) + jnp.sum(s, axis=-1)
    hcell = num / jnp.maximum(jnp.abs(den), jnp.exp(-m_comb))[..., None]
    return hcell.reshape(b, nh, s_len, dv)


def mlstm_mixer(hn, w_in, b_gate, mh_gain, w_out):
    b, s_len, _ = hn.shape
    idx = np.cumsum(ML_SPLITS)[:-1].tolist()
    q, k, v, o, gi, gf = jnp.split(hn @ w_in, idx, axis=-1)
    heads = lambda t, d: t.reshape(b, s_len, ML_HEADS, d).transpose(0, 2, 1, 3).astype(jnp.float32)
    q, k, v = heads(q, ML_DQK), heads(k, ML_DQK), heads(v, ML_DV)
    cap = lambda t: GATE_CAP * jnp.tanh(t / GATE_CAP)
    b_i, b_f = b_gate[:ML_HEADS], b_gate[ML_HEADS:]
    logi = cap(gi.astype(jnp.float32) + b_i).transpose(0, 2, 1)
    logf = jax.nn.log_sigmoid(cap(gf.astype(jnp.float32) + b_f)).transpose(0, 2, 1)
    hcell = mlstm_chunkwise(q, k, v, logi, logf)
    hcell = rmsnorm(hcell.transpose(0, 2, 1, 3), mh_gain).reshape(b, s_len, ML_HEADS * ML_DV)
    return ((jax.nn.sigmoid(o.astype(jnp.float32)) * hcell).astype(hn.dtype)) @ w_out


def shared_kv(h, kv_norm, w_kv, k_norm):
    b, s_len, _ = h.shape
    k, v = jnp.split(rmsnorm(h, kv_norm) @ w_kv, 2, axis=-1)
    k = k.reshape(b, s_len, MB_KV_HEADS, MB_HD).transpose(0, 2, 1, 3)
    v = v.reshape(b, s_len, MB_KV_HEADS, MB_HD).transpose(0, 2, 1, 3)
    k = rope_partial(rmsnorm(k, k_norm))
    nb = -(-s_len // MB_BLOCK)
    pad = ((0, 0), (0, 0), (0, nb * MB_BLOCK - s_len), (0, 0))
    k_blocks = jnp.pad(k, pad).reshape(b, MB_KV_HEADS, nb, MB_BLOCK, MB_HD)
    v_blocks = jnp.pad(v, pad).reshape(b, MB_KV_HEADS, nb, MB_BLOCK, MB_HD)
    k_mean = jnp.mean(k_blocks.astype(jnp.float32), axis=3)
    return k_blocks, v_blocks, k_mean


def moba_mixer(hn, w_q, q_norm, w_o, k_blocks, v_blocks, k_mean):
    b, s_len, _ = hn.shape
    nb = k_blocks.shape[2]
    scale = MB_HD ** -0.5
    q = (hn @ w_q).reshape(b, s_len, MB_HEADS, MB_HD).transpose(0, 2, 1, 3)
    q = rope_partial(rmsnorm(q, q_norm))
    gate = jnp.einsum('bkgsd,bknd->bkgsn', q.reshape(b, MB_KV_HEADS, MB_GROUP, s_len, MB_HD).astype(jnp.float32),
                      k_mean).reshape(b, MB_HEADS, s_len, nb)
    qblk = jnp.arange(s_len) // MB_BLOCK
    past = jnp.arange(nb)[None, :] < qblk[:, None]
    gate = jnp.where(past, gate, -jnp.inf)
    if nb < MB_TOPK:
        gate = jnp.concatenate([gate, jnp.full((b, MB_HEADS, s_len, MB_TOPK - nb), -jnp.inf, gate.dtype)], axis=-1)
    sel = jnp.minimum(lax.top_k(gate, MB_TOPK)[1], nb - 1)
    bi = jnp.arange(b)[:, None, None, None]
    gi = (jnp.arange(MB_HEADS) // MB_GROUP)[None, :, None, None]

    def chunk(c):
        t0 = c * MB_QCHUNK
        qc = lax.dynamic_slice_in_dim(q, t0, MB_QCHUNK, axis=2)
        selc = lax.dynamic_slice_in_dim(sel, t0, MB_QCHUNK, axis=2)
        tpos = t0 + jnp.arange(MB_QCHUNK)
        ob = t0 // MB_BLOCK
        k_sel = k_blocks[bi, gi, selc]
        v_sel = v_blocks[bi, gi, selc]
        s_sel = jnp.einsum('bhqd,bhqjud->bhqju', qc, k_sel).astype(jnp.float32) * scale
        valid = jnp.arange(MB_TOPK)[None, :] < (tpos // MB_BLOCK)[:, None]
        s_sel = jnp.where(valid[None, None, :, :, None], s_sel, -jnp.inf).reshape(b, MB_HEADS, MB_QCHUNK, MB_TOPK * MB_BLOCK)
        k_own = lax.dynamic_index_in_dim(k_blocks, ob, axis=2, keepdims=False)
        v_own = lax.dynamic_index_in_dim(v_blocks, ob, axis=2, keepdims=False)
        qg = qc.reshape(b, MB_KV_HEADS, MB_GROUP, MB_QCHUNK, MB_HD)
        s_own = jnp.einsum('bkgqd,bkud->bkgqu', qg, k_own).astype(jnp.float32).reshape(b, MB_HEADS, MB_QCHUNK, MB_BLOCK) * scale
        kpos = ob * MB_BLOCK + jnp.arange(MB_BLOCK)
        s_own = jnp.where(kpos[None, :] <= tpos[:, None], s_own, -jnp.inf)
        prob = jax.nn.softmax(jnp.concatenate([s_sel, s_own], axis=-1), axis=-1)
        p_sel = prob[..., :MB_TOPK * MB_BLOCK].reshape(b, MB_HEADS, MB_QCHUNK, MB_TOPK, MB_BLOCK).astype(v_sel.dtype)
        p_own = prob[..., MB_TOPK * MB_BLOCK:].reshape(b, MB_KV_HEADS, MB_GROUP, MB_QCHUNK, MB_BLOCK).astype(v_own.dtype)
        o_sel = jnp.einsum('bhqju,bhqjud->bhqd', p_sel, v_sel)
        o_own = jnp.einsum('bkgqu,bkud->bkgqd', p_own, v_own).reshape(b, MB_HEADS, MB_QCHUNK, MB_HD)
        return o_sel + o_own

    out = lax.map(chunk, jnp.arange(s_len // MB_QCHUNK))
    out = jnp.transpose(out, (1, 0, 3, 2, 4)).reshape(b, s_len, MB_HEADS * MB_HD)
    return out @ w_o


def setup_inputs(seed: int = 0) -> dict:
    key = jax.random.key(seed)
    ks = jax.random.split(key, 24)
    f32 = jnp.float32
    nrm = lambda k, shape, fan: jax.random.normal(k, shape, f32) * (fan ** -0.5)
    gain = lambda k, shape: 1.0 + 0.05 * jax.random.normal(k, shape, f32)
    b_gate = jnp.concatenate([
        0.1 * jax.random.normal(ks[4], (N_A_LAYERS, ML_HEADS), f32),
        jnp.linspace(3.0, 6.0, ML_HEADS, dtype=f32)[None, :] + 0.1 * jax.random.normal(ks[5], (N_A_LAYERS, ML_HEADS), f32)], axis=-1)
    return {
        "x": jax.random.normal(ks[0], (BATCH, SEQ, D_MODEL), f32),
        "p": jax.random.normal(ks[1], (DEPTH, BATCH, SEQ, PLE_DIM), f32),
        "norm_mix": gain(ks[2], (DEPTH, D_MODEL)),
        "a_w_in": nrm(ks[3], (N_A_LAYERS, D_MODEL, ML_IN), D_MODEL),
        "a_b_gate": b_gate,
        "a_mh_gain": gain(ks[6], (N_A_LAYERS, ML_DV)),
        "a_w_out": nrm(ks[7], (N_A_LAYERS, ML_HEADS * ML_DV, D_MODEL), ML_HEADS * ML_DV),
        "kv_norm": gain(ks[8], (D_MODEL,)),
        "w_kv": nrm(ks[9], (D_MODEL, 2 * MB_KV_HEADS * MB_HD), D_MODEL),
        "k_norm": gain(ks[10], (MB_HD,)),
        "b_w_q": nrm(ks[11], (N_B_LAYERS, D_MODEL, MB_HEADS * MB_HD), D_MODEL),
        "b_q_norm": gain(ks[12], (N_B_LAYERS, MB_HD)),
        "b_w_o": nrm(ks[13], (N_B_LAYERS, MB_HEADS * MB_HD, D_MODEL), MB_HEADS * MB_HD),
        "norm_ffn": gain(ks[14], (DEPTH, D_MODEL)),
        "w_gate_up": nrm(ks[15], (DEPTH, D_MODEL, 2 * FFN_HIDDEN), D_MODEL),
        "w_down": nrm(ks[16], (DEPTH, FFN_HIDDEN, D_MODEL), FFN_HIDDEN),
        "norm_ple": gain(ks[17], (DEPTH, D_MODEL)),
        "w_ple_gate": nrm(ks[18], (DEPTH, D_MODEL, D_MODEL), D_MODEL),
        "w_ple_up": nrm(ks[19], (DEPTH, PLE_DIM, D_MODEL), PLE_DIM),
    }


def reference(x, p, norm_mix, a_w_in, a_b_gate, a_mh_gain, a_w_out, kv_norm, w_kv, k_norm,
              b_w_q, b_q_norm, b_w_o, norm_ffn, w_gate_up, w_down, norm_ple, w_ple_gate, w_ple_up):
    h = x
    shared = None
    for i in range(DEPTH):
        hn = rmsnorm(h, norm_mix[i])
        if i < N_A_LAYERS:
            h = h + mlstm_mixer(hn, a_w_in[i], a_b_gate[i], a_mh_gain[i], a_w_out[i])
        else:
            if shared is None:
                shared = shared_kv(h, kv_norm, w_kv, k_norm)
            j = i - N_A_LAYERS
            h = h + moba_mixer(hn, b_w_q[j], b_q_norm[j], b_w_o[j], *shared)
        h = h + swiglu(rmsnorm(h, norm_ffn[i]), w_gate_up[i], w_down[i])
        gate = jax.nn.sigmoid(rmsnorm(h, norm_ple[i]) @ w_ple_gate[i])
        h = h + (p[i].astype(h.dtype) @ w_ple_up[i]) * gate
    return h
```

```python
import functools

import jax
import jax.numpy as jnp
from jax import lax
from jax.experimental import pallas as pl
from jax.experimental.pallas import tpu as pltpu

D_MODEL = 1024
PLE_DIM = 256
EPS = 1e-6

ML_HEADS = 8
ML_DV = 128
ML_DQK = 64
GATE_CAP = 15.0
ML_QK_COLS = ML_HEADS * ML_DQK
ML_V_COLS = ML_HEADS * ML_DV
ML_CHUNK = 256

MB_HEADS = 16
MB_KV_HEADS = 4
MB_HD = 64
MB_GROUP = MB_HEADS // MB_KV_HEADS
MB_BLOCK = 256
MB_TOPK = 3
ROPE_THETA = 500000.0
ROPE_DIM = MB_HD // 4
ROPE_HALF = ROPE_DIM // 2

FFN_HIDDEN = 2816

LANES = 128
GATE_LANES = LANES
NEG_BIG = -1e30
VMEM_LIMIT = 56 * 1024 * 1024

ROW_TILE = 512
POST_ROW_TILE = 256


def _bf16(x):
    return x.astype(jnp.bfloat16)


def _dot(a, b):
    return jnp.dot(a, b, preferred_element_type=jnp.float32)


def _dot_nt(a, b):
    return lax.dot_general(a, b, (((1,), (1,)), ((), ())), preferred_element_type=jnp.float32)


def _dot_tn(a, b):
    return lax.dot_general(a, b, (((0,), (0,)), ((), ())), preferred_element_type=jnp.float32)


def _rms_rows(x):
    return x * lax.rsqrt(jnp.mean(x * x, axis=-1, keepdims=True) + EPS)


def _const_spec(shape):
    nd = len(shape)
    return pl.BlockSpec(shape, lambda *_: (0,) * nd, pipeline_mode=pl.Buffered(1))


def _params(sem):
    return pltpu.CompilerParams(dimension_semantics=sem, vmem_limit_bytes=VMEM_LIMIT)


def _inproj_kernel(x_ref, g_ref, w_ref, wg_ref, bg_ref, q_ref, k_ref, v_ref, o_ref, gate_ref):
    hn = _bf16(_rms_rows(x_ref[...]) * g_ref[...])
    c0, c1, c2, c3 = ML_QK_COLS, 2 * ML_QK_COLS, 2 * ML_QK_COLS + ML_V_COLS, 2 * ML_QK_COLS + 2 * ML_V_COLS
    q_ref[...] = _bf16(_dot(hn, w_ref[:, 0:c0]) * (ML_DQK ** -0.5))
    k_ref[...] = _bf16(_dot(hn, w_ref[:, c0:c1]))
    v_ref[...] = _bf16(_dot(hn, w_ref[:, c1:c2]))
    o_ref[...] = _bf16(_dot(hn, w_ref[:, c2:c3]))

    pre = _dot(hn, wg_ref[...]) + bg_ref[...]
    cap = GATE_CAP * jnp.tanh(pre * (1.0 / GATE_CAP))
    logf = jnp.minimum(cap, 0.0) - jnp.log1p(jnp.exp(-jnp.abs(cap)))
    rows = cap.shape[0]
    row_in_chunk = lax.broadcasted_iota(jnp.int32, cap.shape, 0) % ML_CHUNK
    cum = logf
    d = 1
    while d < ML_CHUNK:
        cum = cum + jnp.where(row_in_chunk >= d, pltpu.roll(cum, d, 0), 0.0)
        d *= 2
    li_shift = pltpu.roll(cap, ML_HEADS, 1)
    parts = []
    for c in range(rows // ML_CHUNK):
        sl = slice(c * ML_CHUNK, (c + 1) * ML_CHUNK)
        g_last = cum[(c + 1) * ML_CHUNK - 1:(c + 1) * ML_CHUNK, :]
        parts.append(g_last - cum[sl] + li_shift[sl])
    wlog = pltpu.roll(jnp.concatenate(parts, axis=0), ML_HEADS, 1)
    lane = lax.broadcasted_iota(jnp.int32, cap.shape, 1)
    gate_ref[...] = jnp.where(lane < ML_HEADS, cap, jnp.where(lane < 2 * ML_HEADS, cum, wlog))


def _inproj(x2, g, w, wg, bg):
    t = x2.shape[0]
    tm = ROW_TILE
    row = lambda n: pl.BlockSpec((tm, n), lambda i: (i, 0))
    return pl.pallas_call(
        _inproj_kernel,
        grid=(t // tm,),
        in_specs=[row(D_MODEL), _const_spec((1, D_MODEL)), _const_spec(w.shape), _const_spec(wg.shape),
                  _const_spec((1, GATE_LANES))],
        out_specs=[row(ML_QK_COLS), row(ML_QK_COLS), row(ML_V_COLS), row(ML_V_COLS), row(GATE_LANES)],
        out_shape=[jax.ShapeDtypeStruct((t, ML_QK_COLS), jnp.bfloat16),
                   jax.ShapeDtypeStruct((t, ML_QK_COLS), jnp.bfloat16),
                   jax.ShapeDtypeStruct((t, ML_V_COLS), jnp.bfloat16),
                   jax.ShapeDtypeStruct((t, ML_V_COLS), jnp.bfloat16),
                   jax.ShapeDtypeStruct((t, GATE_LANES), jnp.float32)],
        compiler_params=_params(("parallel",)),
        name="inproj",
    )(x2, g, w, wg, bg)


def _mlstm_kernel(q_ref, k_ref, v_ref, o_ref, gcol_ref, grow_ref, gain_ref, y_ref, c_ref, n_ref):
    h = pl.program_id(1)
    c = pl.program_id(2)
    L = ML_CHUNK

    @pl.when(c == 0)
    def _():
        c_ref[...] = jnp.zeros_like(c_ref)
        n_ref[...] = jnp.zeros_like(n_ref)

    lane_row = lax.broadcasted_iota(jnp.int32, (1, LANES), 1)
    head_lanes = (lane_row // ML_DQK) == (h % 2)
    q = jnp.where(head_lanes, q_ref[...], jnp.zeros((), jnp.bfloat16))
    k = k_ref[...]
    v = v_ref[...]

    gc = gcol_ref[...]
    lane = lax.broadcasted_iota(jnp.int32, gc.shape, 1)
    g_col = jnp.sum(jnp.where(lane == ML_HEADS + h, gc, 0.0), axis=-1, keepdims=True)
    wlog_col = jnp.sum(jnp.where(lane == 2 * ML_HEADS + h, gc, 0.0), axis=-1, keepdims=True)
    li_row = grow_ref[0, pl.ds(h, 1), :]
    g_row = grow_ref[0, pl.ds(ML_HEADS + h, 1), :]
    decay = jnp.exp(g_row[:, L - 1:L])

    r_i = lax.broadcasted_iota(jnp.int32, (L, L), 0)
    c_i = lax.broadcasted_iota(jnp.int32, (L, L), 1)
    dmat = jnp.where(r_i >= c_i, g_col - g_row + li_row, NEG_BIG)
    p = _dot_nt(q, k) * jnp.exp(dmat)

    eg = jnp.exp(g_col)
    c_prev = c_ref[...]
    n_prev = n_ref[...]
    qf = q.astype(jnp.float32)
    num = _dot(_bf16(p), v) + _dot(q, _bf16(c_prev)) * eg
    den = jnp.sum(p, axis=-1, keepdims=True) + jnp.sum(qf * n_prev, axis=-1, keepdims=True) * eg
    hc = num / jnp.maximum(jnp.abs(den), 1.0)

    hn = _rms_rows(hc) * gain_ref[...]
    y_ref[...] = _bf16(jax.nn.sigmoid(o_ref[...].astype(jnp.float32)) * hn)

    kw = k.astype(jnp.float32) * jnp.exp(wlog_col)
    c_ref[...] = decay * c_prev + _dot_tn(_bf16(kw), v)
    n_ref[...] = decay * n_prev + jnp.sum(kw, axis=0, keepdims=True)


def _mlstm(q, k, v, o, gcol, grow, gain, batch, seq):
    L = ML_CHUNK
    nc = seq // L
    t = batch * seq
    rowblk = lambda b, h, c: b * nc + c
    return pl.pallas_call(
        _mlstm_kernel,
        grid=(batch, ML_HEADS, nc),
        in_specs=[pl.BlockSpec((L, LANES), lambda b, h, c: (rowblk(b, h, c), h // 2)),
                  pl.BlockSpec((L, LANES), lambda b, h, c: (rowblk(b, h, c), h // 2)),
                  pl.BlockSpec((L, ML_DV), lambda b, h, c: (rowblk(b, h, c), h)),
                  pl.BlockSpec((L, ML_DV), lambda b, h, c: (rowblk(b, h, c), h)),
                  pl.BlockSpec((L, GATE_LANES), lambda b, h, c: (rowblk(b, h, c), 0)),
                  pl.BlockSpec((1, 3 * ML_HEADS, L), lambda b, h, c: (b, 0, c)),
                  pl.BlockSpec((1, ML_DV), lambda b, h, c: (0, 0))],
        out_specs=pl.BlockSpec((L, ML_DV), lambda b, h, c: (rowblk(b, h, c), h)),
        out_shape=jax.ShapeDtypeStruct((t, ML_V_COLS), jnp.bfloat16),
        scratch_shapes=[pltpu.VMEM((LANES, ML_DV), jnp.float32), pltpu.VMEM((1, LANES), jnp.float32)],
        compiler_params=_params(("parallel", "parallel", "arbitrary")),
        name="mlstm",
    )(q, k, v, o, gcol, grow, gain)


def _post_kernel(h_ref, y_ref, p_ref, wo_ref, gf_ref, wgu_ref, wd_ref, gp_ref, wpg_ref, wpu_ref, out_ref):
    h1 = h_ref[...] + _dot(y_ref[...], wo_ref[...])
    hn = _bf16(_rms_rows(h1) * gf_ref[...])
    gu = _dot(hn, wgu_ref[...])
    act = _bf16(jax.nn.silu(gu[:, :FFN_HIDDEN]) * gu[:, FFN_HIDDEN:])
    h2 = h1 + _dot(act, wd_ref[...])
    hp = _bf16(_rms_rows(h2) * gp_ref[...])
    gate = jax.nn.sigmoid(_dot(hp, wpg_ref[...]))
    up = _dot(_bf16(p_ref[...]), wpu_ref[...])
    out_ref[...] = h2 + up * gate


def _post(h, y, p, wo, gf, wgu, wd, gp, wpg, wpu):
    t = h.shape[0]
    tm = POST_ROW_TILE
    row = lambda n: pl.BlockSpec((tm, n), lambda i: (i, 0))
    return pl.pallas_call(
        _post_kernel,
        grid=(t // tm,),
        in_specs=[row(D_MODEL), row(D_MODEL), row(PLE_DIM), _const_spec(wo.shape), _const_spec((1, D_MODEL)),
                  _const_spec(wgu.shape), _const_spec(wd.shape), _const_spec((1, D_MODEL)),
                  _const_spec(wpg.shape), _const_spec(wpu.shape)],
        out_specs=row(D_MODEL),
        out_shape=jax.ShapeDtypeStruct((t, D_MODEL), jnp.float32),
        compiler_params=_params(("parallel",)),
        name="post",
    )(h, y, p, wo, gf, wgu, wd, gp, wpg, wpu)


def _head_norm_rope(x, gain, ones_bd, cos_t, sin_lo, sin_hi):
    sq = x * x
    hi = _bf16(sq)
    lo = _bf16(sq - hi.astype(jnp.float32))
    ms = (_dot(hi, ones_bd) + _dot(lo, ones_bd)) * (1.0 / MB_HD)
    xn = x * lax.rsqrt(ms + EPS) * gain
    outs = []
    for s in range(x.shape[1] // LANES):
        xs = xn[:, s * LANES:(s + 1) * LANES]
        outs.append(xs * cos_t + pltpu.roll(xs, LANES - ROPE_HALF, 1) * sin_lo + pltpu.roll(xs, ROPE_HALF, 1) * sin_hi)
    return jnp.concatenate(outs, axis=1)


def _proj_kernel(h_ref, gq_ref, gkv_ref, wq_ref, wkv_ref, qn_ref, kn_ref, bd_ref, cos_ref, slo_ref, shi_ref,
                 q_ref, k_ref, v_ref, km_ref):
    r = _rms_rows(h_ref[...])
    hq = _bf16(r * gq_ref[...])
    hkv = _bf16(r * gkv_ref[...])
    cos_t, sin_lo, sin_hi = cos_ref[...], slo_ref[...], shi_ref[...]
    bd = bd_ref[...]
    kvw = MB_KV_HEADS * MB_HD
    for s in range(D_MODEL // kvw):
        cols = slice(s * kvw, (s + 1) * kvw)
        qs = _dot(hq, wq_ref[:, cols])
        qs = _head_norm_rope(qs, qn_ref[:, cols], bd, cos_t, sin_lo, sin_hi)
        q_ref[:, cols] = _bf16(qs * (MB_HD ** -0.5))
    ks = _head_norm_rope(_dot(hkv, wkv_ref[:, :kvw]), kn_ref[...], bd, cos_t, sin_lo, sin_hi)
    k_ref[...] = _bf16(ks)
    v_ref[...] = _bf16(_dot(hkv, wkv_ref[:, kvw:]))
    for b in range(ks.shape[0] // MB_BLOCK):
        km_ref[b] = jnp.mean(ks[b * MB_BLOCK:(b + 1) * MB_BLOCK], axis=0, keepdims=True)


def _proj(h, gq, gkv, wq, wkv, qn, kn, bd, cos_t, sin_lo, sin_hi, seq):
    t = h.shape[0]
    tm = ROW_TILE
    kvw = MB_KV_HEADS * MB_HD
    row = lambda n: pl.BlockSpec((tm, n), lambda i: (i, 0))
    tab = pl.BlockSpec((tm, LANES), lambda i: (i % (seq // tm), 0))
    return pl.pallas_call(
        _proj_kernel,
        grid=(t // tm,),
        in_specs=[row(D_MODEL), _const_spec((1, D_MODEL)), _const_spec((1, D_MODEL)), _const_spec(wq.shape),
                  _const_spec(wkv.shape), _const_spec((1, D_MODEL)), _const_spec((1, kvw)), _const_spec(bd.shape),
                  tab, tab, tab],
        out_specs=[row(D_MODEL), row(kvw), row(kvw),
                   pl.BlockSpec((tm // MB_BLOCK, 1, kvw), lambda i: (i, 0, 0))],
        out_shape=[jax.ShapeDtypeStruct((t, D_MODEL), jnp.bfloat16),
                   jax.ShapeDtypeStruct((t, kvw), jnp.bfloat16),
                   jax.ShapeDtypeStruct((t, kvw), jnp.bfloat16),
                   jax.ShapeDtypeStruct((t // MB_BLOCK, 1, kvw), jnp.float32)],
        compiler_params=_params(("parallel",)),
        name="proj",
    )(h, gq, gkv, wq, wkv, qn, kn, bd, cos_t, sin_lo, sin_hi)


def _moba_kernel(q_ref, k_ref, vt_ref, km_ref, o_ref, sel_ref, m_ref, l_ref, acc_ref):
    i = pl.program_id(2)
    nb = km_ref.shape[2]
    km = km_ref[0, 0]
    km_hi = _bf16(km)
    km_lo = _bf16(km - km_hi.astype(jnp.float32))
    blk = lax.broadcasted_iota(jnp.int32, (nb, MB_BLOCK), 0)
    past = blk < i
    key_i = lax.broadcasted_iota(jnp.int32, (MB_BLOCK, MB_BLOCK), 0)
    qry_i = lax.broadcasted_iota(jnp.int32, (MB_BLOCK, MB_BLOCK), 1)
    causal = key_i <= qry_i

    k_own = k_ref[0, 0, i]
    vt_own = vt_ref[0, 0, i]
    for h in range(MB_GROUP):
        q = q_ref[0, h]
        gate = jnp.where(past, _dot_nt(km_hi, q) + _dot_nt(km_lo, q), -jnp.inf)
        rank = jnp.zeros((nb, MB_BLOCK), jnp.int32)
        for m in range(nb):
            gm = gate[m:m + 1, :]
            beats = (gm > gate) | ((gm == gate) & (m < blk))
            rank = rank + beats.astype(jnp.int32)
        sel_ref[h] = ((rank < MB_TOPK) & past).astype(jnp.float32)

        s = jnp.where(causal, _dot_nt(k_own, q), NEG_BIG)
        mx = jnp.max(s, axis=0, keepdims=True)
        p = jnp.exp(s - mx)
        m_ref[pl.ds(h, 1), :] = mx
        l_ref[pl.ds(h, 1), :] = jnp.sum(p, axis=0, keepdims=True)
        acc_ref[h * MB_HD:(h + 1) * MB_HD, :] = _dot(vt_own, _bf16(p))

    def body(n, carry):
        kn = k_ref[0, 0, n]
        vtn = vt_ref[0, 0, n]
        for h in range(MB_GROUP):
            q = q_ref[0, h]
            s = _dot_nt(kn, q)
            chosen = sel_ref[h, pl.ds(n, 1), :] > 0.5
            m_old = m_ref[pl.ds(h, 1), :]
            m_new = jnp.maximum(m_old, jnp.where(chosen, jnp.max(s, axis=0, keepdims=True), NEG_BIG))
            alpha = jnp.exp(m_old - m_new)
            p = jnp.exp(s - jnp.where(chosen, m_new, -NEG_BIG))
            m_ref[pl.ds(h, 1), :] = m_new
            l_ref[pl.ds(h, 1), :] = alpha * l_ref[pl.ds(h, 1), :] + jnp.sum(p, axis=0, keepdims=True)
            rows = slice(h * MB_HD, (h + 1) * MB_HD)
            acc_ref[rows, :] = alpha * acc_ref[rows, :] + _dot(vtn, _bf16(p))
        return carry

    lax.fori_loop(0, i, body, 0)

    for h in range(MB_GROUP):
        rows = slice(h * MB_HD, (h + 1) * MB_HD)
        acc_ref[rows, :] = acc_ref[rows, :] / l_ref[pl.ds(h, 1), :]
    o_ref[...] = _bf16(acc_ref[...].T)


def _moba(q, k, vt, km, batch, seq):
    nb = seq // MB_BLOCK
    t = batch * seq
    gw = MB_GROUP * MB_HD
    return pl.pallas_call(
        _moba_kernel,
        grid=(batch, MB_KV_HEADS, nb),
        in_specs=[pl.BlockSpec((1, MB_GROUP, MB_BLOCK, MB_HD), lambda b, g, i: (b, g, i, 0)),
                  pl.BlockSpec((1, 1, nb, MB_BLOCK, MB_HD), lambda b, g, i: (b, g, 0, 0, 0)),
                  pl.BlockSpec((1, 1, nb, MB_HD, MB_BLOCK), lambda b, g, i: (b, g, 0, 0, 0)),
                  pl.BlockSpec((1, 1, nb, MB_HD), lambda b, g, i: (b, g, 0, 0))],
        out_specs=pl.BlockSpec((MB_BLOCK, gw), lambda b, g, i: (b * nb + i, g)),
        out_shape=jax.ShapeDtypeStruct((t, MB_HEADS * MB_HD), jnp.bfloat16),
        scratch_shapes=[pltpu.VMEM((MB_GROUP, nb, MB_BLOCK), jnp.float32),
                        pltpu.VMEM((8, MB_BLOCK), jnp.float32),
                        pltpu.VMEM((8, MB_BLOCK), jnp.float32),
                        pltpu.VMEM((gw, MB_BLOCK), jnp.float32)],
        compiler_params=_params(("parallel", "parallel", "arbitrary")),
        name="moba",
    )(q, k, vt, km)


def _rope_tables(seq):
    inv = ROPE_THETA ** (-jnp.arange(0, ROPE_DIM, 2, dtype=jnp.float32) / ROPE_DIM)
    ang = jnp.arange(seq, dtype=jnp.float32)[:, None] * inv[None, :]
    cos, sin = jnp.cos(ang), jnp.sin(ang)
    zeros = jnp.zeros((seq, MB_HD - ROPE_DIM), jnp.float32)
    zh = jnp.zeros((seq, ROPE_HALF), jnp.float32)
    cos_h = jnp.concatenate([cos, cos, jnp.ones_like(zeros)], axis=1)
    lo_h = jnp.concatenate([-sin, zh, zeros], axis=1)
    hi_h = jnp.concatenate([zh, sin, zeros], axis=1)
    two = lambda a: jnp.concatenate([a, a], axis=1)
    return two(cos_h), two(lo_h), two(hi_h)


def kernel(x, p, norm_mix, a_w_in, a_b_gate, a_mh_gain, a_w_out, kv_norm, w_kv, k_norm, b_w_q, b_q_norm, b_w_o,
           norm_ffn, w_gate_up, w_down, norm_ple, w_ple_gate, w_ple_up):
    batch, seq, d = x.shape
    t = batch * seq
    f32 = jnp.float32
    x2 = x.reshape(t, d)
    p2 = p.reshape(p.shape[0], t, PLE_DIM)
    row = lambda a: a.reshape(1, -1).astype(f32)

    w_in = a_w_in[0]
    n_main = 2 * ML_QK_COLS + 2 * ML_V_COLS
    w_main = _bf16(w_in[:, :n_main])
    w_gate = _bf16(jnp.pad(w_in[:, n_main:], ((0, 0), (0, GATE_LANES - 2 * ML_HEADS))))
    b_gate = jnp.pad(a_b_gate[0].astype(f32), (0, GATE_LANES - 2 * ML_HEADS)).reshape(1, GATE_LANES)
    q, k, v, o, gcol = _inproj(x2, row(norm_mix[0]), w_main, w_gate, b_gate)
    grow = jnp.transpose(gcol.reshape(batch, seq, GATE_LANES)[:, :, :3 * ML_HEADS], (0, 2, 1))
    y = _mlstm(q, k, v, o, gcol, grow, row(a_mh_gain[0]), batch, seq)
    h = _post(x2, y, p2[0], _bf16(a_w_out[0]), row(norm_ffn[0]), _bf16(w_gate_up[0]), _bf16(w_down[0]),
              row(norm_ple[0]), _bf16(w_ple_gate[0]), _bf16(w_ple_up[0]))

    nb = seq // MB_BLOCK
    kvw = MB_KV_HEADS * MB_HD
    cos_t, sin_lo, sin_hi = _rope_tables(seq)
    head = lax.broadcasted_iota(jnp.int32, (kvw, kvw), 0) // MB_HD
    ones_bd = (head == head.T).astype(jnp.bfloat16)
    qn = row(jnp.tile(b_q_norm[0], MB_HEADS))
    kn = row(jnp.tile(k_norm, MB_KV_HEADS))
    q, k, v, km = _proj(h, row(norm_mix[1]), row(kv_norm), _bf16(b_w_q[0]), _bf16(w_kv), qn, kn, ones_bd,
                        cos_t, sin_lo, sin_hi, seq)
    qh = jnp.transpose(q.reshape(batch, seq, MB_HEADS, MB_HD), (0, 2, 1, 3))
    kh = jnp.transpose(k.reshape(batch, nb, MB_BLOCK, MB_KV_HEADS, MB_HD), (0, 3, 1, 2, 4))
    vth = jnp.transpose(v.reshape(batch, nb, MB_BLOCK, MB_KV_HEADS, MB_HD), (0, 3, 1, 4, 2))
    kmh = jnp.transpose(km.reshape(batch, nb, MB_KV_HEADS, MB_HD), (0, 2, 1, 3))
    y = _moba(qh, kh, vth, kmh, batch, seq)
    h = _post(h, y, p2[1], _bf16(b_w_o[0]), row(norm_ffn[1]), _bf16(w_gate_up[1]), _bf16(w_down[1]),
              row(norm_ple[1]), _bf16(w_ple_gate[1]), _bf16(w_ple_up[1]))
    return h.reshape(batch, seq, d)
```

```python
import functools

import jax
import jax.numpy as jnp
from jax import lax
from jax.experimental import pallas as pl
from jax.experimental.pallas import tpu as pltpu

D_MODEL = 1024
PLE_DIM = 256
EPS = 1e-6

ML_HEADS = 8
ML_DV = 128
ML_DQK = 64
GATE_CAP = 15.0
ML_QK_COLS = ML_HEADS * ML_DQK
ML_V_COLS = ML_HEADS * ML_DV
ML_CHUNK = 256

MB_HEADS = 16
MB_KV_HEADS = 4
MB_HD = 64
MB_GROUP = MB_HEADS // MB_KV_HEADS
MB_BLOCK = 256
MB_TOPK = 3
ROPE_THETA = 500000.0
ROPE_DIM = MB_HD // 4
ROPE_HALF = ROPE_DIM // 2

FFN_HIDDEN = 2816

LANES = 128
GATE_LANES = LANES
NEG_BIG = -1e30
VMEM_LIMIT = 56 * 1024 * 1024

ROW_TILE = 512
POST_ROW_TILE = 256


def _bf16(x):
    return x.astype(jnp.bfloat16)


def _dot(a, b):
    return jnp.dot(a, b, preferred_element_type=jnp.float32)


def _dot_nt(a, b):
    return lax.dot_general(a, b, (((1,), (1,)), ((), ())), preferred_element_type=jnp.float32)


def _dot_tn(a, b):
    return lax.dot_general(a, b, (((0,), (0,)), ((), ())), preferred_element_type=jnp.float32)


def _rms_rows(x):
    return x * lax.rsqrt(jnp.mean(x * x, axis=-1, keepdims=True) + EPS)


def _const_spec(shape):
    nd = len(shape)
    return pl.BlockSpec(shape, lambda *_: (0,) * nd, pipeline_mode=pl.Buffered(1))


def _params(sem):
    return pltpu.CompilerParams(dimension_semantics=sem, vmem_limit_bytes=VMEM_LIMIT)


def _inproj_kernel(x_ref, g_ref, w_ref, wg_ref, bg_ref, q_ref, k_ref, v_ref, o_ref, gate_ref):
    hn = _bf16(_rms_rows(x_ref[...]) * g_ref[...])
    c0, c1, c2, c3 = ML_QK_COLS, 2 * ML_QK_COLS, 2 * ML_QK_COLS + ML_V_COLS, 2 * ML_QK_COLS + 2 * ML_V_COLS
    q_ref[...] = _bf16(_dot(hn, w_ref[:, 0:c0]) * (ML_DQK ** -0.5))
    k_ref[...] = _bf16(_dot(hn, w_ref[:, c0:c1]))
    v_ref[...] = _bf16(_dot(hn, w_ref[:, c1:c2]))
    o_ref[...] = _bf16(_dot(hn, w_ref[:, c2:c3]))

    pre = _dot(hn, wg_ref[...]) + bg_ref[...]
    cap = GATE_CAP * jnp.tanh(pre * (1.0 / GATE_CAP))
    logf = jnp.minimum(cap, 0.0) - jnp.log1p(jnp.exp(-jnp.abs(cap)))
    rows = cap.shape[0]
    row_in_chunk = lax.broadcasted_iota(jnp.int32, cap.shape, 0) % ML_CHUNK
    cum = logf
    d = 1
    while d < ML_CHUNK:
        cum = cum + jnp.where(row_in_chunk >= d, pltpu.roll(cum, d, 0), 0.0)
        d *= 2
    li_shift = pltpu.roll(cap, ML_HEADS, 1)
    parts = []
    for c in range(rows // ML_CHUNK):
        sl = slice(c * ML_CHUNK, (c + 1) * ML_CHUNK)
        g_last = cum[(c + 1) * ML_CHUNK - 1:(c + 1) * ML_CHUNK, :]
        parts.append(g_last - cum[sl] + li_shift[sl])
    wlog = pltpu.roll(jnp.concatenate(parts, axis=0), ML_HEADS, 1)
    lane = lax.broadcasted_iota(jnp.int32, cap.shape, 1)
    gate_ref[...] = jnp.where(lane < ML_HEADS, cap, jnp.where(lane < 2 * ML_HEADS, cum, wlog))


def _inproj(x2, g, w, wg, bg):
    t = x2.shape[0]
    tm = ROW_TILE
    row = lambda n: pl.BlockSpec((tm, n), lambda i: (i, 0))
    return pl.pallas_call(
        _inproj_kernel,
        grid=(t // tm,),
        in_specs=[row(D_MODEL), _const_spec((1, D_MODEL)), _const_spec(w.shape), _const_spec(wg.shape),
                  _const_spec((1, GATE_LANES))],
        out_specs=[row(ML_QK_COLS), row(ML_QK_COLS), row(ML_V_COLS), row(ML_V_COLS), row(GATE_LANES)],
        out_shape=[jax.ShapeDtypeStruct((t, ML_QK_COLS), jnp.bfloat16),
                   jax.ShapeDtypeStruct((t, ML_QK_COLS), jnp.bfloat16),
                   jax.ShapeDtypeStruct((t, ML_V_COLS), jnp.bfloat16),
                   jax.ShapeDtypeStruct((t, ML_V_COLS), jnp.bfloat16),
                   jax.ShapeDtypeStruct((t, GATE_LANES), jnp.float32)],
        compiler_params=_params(("parallel",)),
        name="inproj",
    )(x2, g, w, wg, bg)


def _mlstm_kernel(q_ref, k_ref, v_ref, o_ref, gcol_ref, grow_ref, gain_ref, y_ref, c_ref, n_ref):
    h = pl.program_id(1)
    c = pl.program_id(2)
    L = ML_CHUNK

    @pl.when(c == 0)
    def _():
        c_ref[...] = jnp.zeros_like(c_ref)
        n_ref[...] = jnp.zeros_like(n_ref)

    lane_row = lax.broadcasted_iota(jnp.int32, (1, LANES), 1)
    head_lanes = (lane_row // ML_DQK) == (h % 2)
    q = jnp.where(head_lanes, q_ref[...], jnp.zeros((), jnp.bfloat16))
    k = k_ref[...]
    v = v_ref[...]

    gc = gcol_ref[...]
    lane = lax.broadcasted_iota(jnp.int32, gc.shape, 1)
    g_col = jnp.sum(jnp.where(lane == ML_HEADS + h, gc, 0.0), axis=-1, keepdims=True)
    wlog_col = jnp.sum(jnp.where(lane == 2 * ML_HEADS + h, gc, 0.0), axis=-1, keepdims=True)
    li_row = grow_ref[0, pl.ds(h, 1), :]
    g_row = grow_ref[0, pl.ds(ML_HEADS + h, 1), :]
    decay = jnp.exp(g_row[:, L - 1:L])

    r_i = lax.broadcasted_iota(jnp.int32, (L, L), 0)
    c_i = lax.broadcasted_iota(jnp.int32, (L, L), 1)
    dmat = jnp.where(r_i >= c_i, g_col - g_row + li_row, NEG_BIG)
    p = _dot_nt(q, k) * jnp.exp(dmat)

    eg = jnp.exp(g_col)
    c_prev = c_ref[...]
    n_prev = n_ref[...]
    qf = q.astype(jnp.float32)
    num = _dot(_bf16(p), v) + _dot(q, _bf16(c_prev)) * eg
    den = jnp.sum(p, axis=-1, keepdims=True) + jnp.sum(qf * n_prev, axis=-1, keepdims=True) * eg
    hc = num / jnp.maximum(jnp.abs(den), 1.0)

    hn = _rms_rows(hc) * gain_ref[...]
    y_ref[...] = _bf16(jax.nn.sigmoid(o_ref[...].astype(jnp.float32)) * hn)

    kw = k.astype(jnp.float32) * jnp.exp(wlog_col)
    c_ref[...] = decay * c_prev + _dot_tn(_bf16(kw), v)
    n_ref[...] = decay * n_prev + jnp.sum(kw, axis=0, keepdims=True)


def _mlstm(q, k, v, o, gcol, grow, gain, batch, seq):
    L = ML_CHUNK
    nc = seq // L
    t = batch * seq
    rowblk = lambda b, h, c: b * nc + c
    return pl.pallas_call(
        _mlstm_kernel,
        grid=(batch, ML_HEADS, nc),
        in_specs=[pl.BlockSpec((L, LANES), lambda b, h, c: (rowblk(b, h, c), h // 2)),
                  pl.BlockSpec((L, LANES), lambda b, h, c: (rowblk(b, h, c), h // 2)),
                  pl.BlockSpec((L, ML_DV), lambda b, h, c: (rowblk(b, h, c), h)),
                  pl.BlockSpec((L, ML_DV), lambda b, h, c: (rowblk(b, h, c), h)),
                  pl.BlockSpec((L, GATE_LANES), lambda b, h, c: (rowblk(b, h, c), 0)),
                  pl.BlockSpec((1, 3 * ML_HEADS, L), lambda b, h, c: (b, 0, c)),
                  pl.BlockSpec((1, ML_DV), lambda b, h, c: (0, 0))],
        out_specs=pl.BlockSpec((L, ML_DV), lambda b, h, c: (rowblk(b, h, c), h)),
        out_shape=jax.ShapeDtypeStruct((t, ML_V_COLS), jnp.bfloat16),
        scratch_shapes=[pltpu.VMEM((LANES, ML_DV), jnp.float32), pltpu.VMEM((1, LANES), jnp.float32)],
        compiler_params=_params(("parallel", "parallel", "arbitrary")),
        name="mlstm",
    )(q, k, v, o, gcol, grow, gain)


def _post_kernel(h_ref, y_ref, p_ref, wo_ref, gf_ref, wgu_ref, wd_ref, gp_ref, wpg_ref, wpu_ref, out_ref):
    h1 = h_ref[...] + _dot(y_ref[...], wo_ref[...])
    hn = _bf16(_rms_rows(h1) * gf_ref[...])
    gu = _dot(hn, wgu_ref[...])
    act = _bf16(jax.nn.silu(gu[:, :FFN_HIDDEN]) * gu[:, FFN_HIDDEN:])
    h2 = h1 + _dot(act, wd_ref[...])
    hp = _bf16(_rms_rows(h2) * gp_ref[...])
    gate = jax.nn.sigmoid(_dot(hp, wpg_ref[...]))
    up = _dot(_bf16(p_ref[...]), wpu_ref[...])
    out_ref[...] = h2 + up * gate


def _post(h, y, p, wo, gf, wgu, wd, gp, wpg, wpu):
    t = h.shape[0]
    tm = POST_ROW_TILE
    row = lambda n: pl.BlockSpec((tm, n), lambda i: (i, 0))
    return pl.pallas_call(
        _post_kernel,
        grid=(t // tm,),
        in_specs=[row(D_MODEL), row(D_MODEL), row(PLE_DIM), _const_spec(wo.shape), _const_spec((1, D_MODEL)),
                  _const_spec(wgu.shape), _const_spec(wd.shape), _const_spec((1, D_MODEL)),
                  _const_spec(wpg.shape), _const_spec(wpu.shape)],
        out_specs=row(D_MODEL),
        out_shape=jax.ShapeDtypeStruct((t, D_MODEL), jnp.float32),
        compiler_params=_params(("parallel",)),
        name="post",
    )(h, y, p, wo, gf, wgu, wd, gp, wpg, wpu)


def _head_norm_rope(x, gain, ones_bd, cos_t, sin_lo, sin_hi):
    sq = x * x
    hi = _bf16(sq)
    lo = _bf16(sq - hi.astype(jnp.float32))
    ms = (_dot(hi, ones_bd) + _dot(lo, ones_bd)) * (1.0 / MB_HD)
    xn = x * lax.rsqrt(ms + EPS) * gain
    outs = []
    for s in range(x.shape[1] // LANES):
        xs = xn[:, s * LANES:(s + 1) * LANES]
        outs.append(xs * cos_t + pltpu.roll(xs, LANES - ROPE_HALF, 1) * sin_lo + pltpu.roll(xs, ROPE_HALF, 1) * sin_hi)
    return jnp.concatenate(outs, axis=1)


def _proj_kernel(h_ref, gq_ref, gkv_ref, wq_ref, wkv_ref, qn_ref, kn_ref, bd_ref, cos_ref, slo_ref, shi_ref,
                 q_ref, k_ref, v_ref, km_ref):
    r = _rms_rows(h_ref[...])
    hq = _bf16(r * gq_ref[...])
    hkv = _bf16(r * gkv_ref[...])
    cos_t, sin_lo, sin_hi = cos_ref[...], slo_ref[...], shi_ref[...]
    bd = bd_ref[...]
    kvw = MB_KV_HEADS * MB_HD
    for s in range(D_MODEL // kvw):
        cols = slice(s * kvw, (s + 1) * kvw)
        qs = _dot(hq, wq_ref[:, cols])
        qs = _head_norm_rope(qs, qn_ref[:, cols], bd, cos_t, sin_lo, sin_hi)
        q_ref[:, cols] = _bf16(qs * (MB_HD ** -0.5))
    ks = _head_norm_rope(_dot(hkv, wkv_ref[:, :kvw]), kn_ref[...], bd, cos_t, sin_lo, sin_hi)
    k_ref[...] = _bf16(ks)
    v_ref[...] = _bf16(_dot(hkv, wkv_ref[:, kvw:]))
    for b in range(ks.shape[0] // MB_BLOCK):
        km_ref[b] = jnp.mean(ks[b * MB_BLOCK:(b + 1) * MB_BLOCK], axis=0, keepdims=True)


def _proj(h, gq, gkv, wq, wkv, qn, kn, bd, cos_t, sin_lo, sin_hi, seq):
    t = h.shape[0]
    tm = ROW_TILE
    kvw = MB_KV_HEADS * MB_HD
    row = lambda n: pl.BlockSpec((tm, n), lambda i: (i, 0))
    tab = pl.BlockSpec((tm, LANES), lambda i: (i % (seq // tm), 0))
    return pl.pallas_call(
        _proj_kernel,
        grid=(t // tm,),
        in_specs=[row(D_MODEL), _const_spec((1, D_MODEL)), _const_spec((1, D_MODEL)), _const_spec(wq.shape),
                  _const_spec(wkv.shape), _const_spec((1, D_MODEL)), _const_spec((1, kvw)), _const_spec(bd.shape),
                  tab, tab, tab],
        out_specs=[row(D_MODEL), row(kvw), row(kvw),
                   pl.BlockSpec((tm // MB_BLOCK, 1, kvw), lambda i: (i, 0, 0))],
        out_shape=[jax.ShapeDtypeStruct((t, D_MODEL), jnp.bfloat16),
                   jax.ShapeDtypeStruct((t, kvw), jnp.bfloat16),
                   jax.ShapeDtypeStruct((t, kvw), jnp.bfloat16),
                   jax.ShapeDtypeStruct((t // MB_BLOCK, 1, kvw), jnp.float32)],
        compiler_params=_params(("parallel",)),
        name="proj",
    )(h, gq, gkv, wq, wkv, qn, kn, bd, cos_t, sin_lo, sin_hi)


def _moba_kernel(q_ref, k_ref, vt_ref, km_ref, o_ref, sel_ref, sa_ref, sb_ref, mxa_ref, mxb_ref, acc_ref):
    i = pl.program_id(2)
    nb = km_ref.shape[2]
    heads = range(MB_GROUP)
    km = km_ref[0, 0]
    km_hi = _bf16(km)
    km_lo = _bf16(km - km_hi.astype(jnp.float32))
    blk = lax.broadcasted_iota(jnp.int32, (nb, MB_BLOCK), 0)
    past = blk < i
    key_i = lax.broadcasted_iota(jnp.int32, (MB_BLOCK, MB_BLOCK), 0)
    qry_i = lax.broadcasted_iota(jnp.int32, (MB_BLOCK, MB_BLOCK), 1)
    causal = key_i <= qry_i

    def scores_into(s_ref, mx_ref, n):
        kn = k_ref[0, 0, n]
        for h in heads:
            s = _dot_nt(kn, q_ref[0, h])
            s_ref[h] = s
            mx_ref[h] = jnp.max(s, axis=0, keepdims=True)

    qs = [q_ref[0, h] for h in heads]
    gates = [jnp.where(past, _dot_nt(km_hi, q) + _dot_nt(km_lo, q), -jnp.inf) for q in qs]
    own = [jnp.where(causal, _dot_nt(k_ref[0, 0, i], q), NEG_BIG) for q in qs]
    scores_into(sa_ref, mxa_ref, 0)
    for h in heads:
        rank = jnp.zeros((nb, MB_BLOCK), jnp.int32)
        for m in range(nb):
            gm = gates[h][m:m + 1, :]
            beats = (gm > gates[h]) | ((gm == gates[h]) & (m < blk))
            rank = rank + beats.astype(jnp.int32)
        sel_ref[h] = ((rank < MB_TOPK) & past).astype(jnp.float32)

    vt_own = vt_ref[0, 0, i]
    ms, ls = [], []
    for h in heads:
        mx = jnp.max(own[h], axis=0, keepdims=True)
        p = jnp.exp(own[h] - mx)
        ms.append(mx)
        ls.append(jnp.sum(p, axis=0, keepdims=True))
        acc_ref[h * MB_HD:(h + 1) * MB_HD, :] = _dot(vt_own, _bf16(p))

    def attend(s_ref, mx_ref, n, carry):
        ms, ls = carry
        vtn = vt_ref[0, 0, n]
        new_ms, new_ls = [], []
        for h in heads:
            chosen = sel_ref[h, pl.ds(n, 1), :] > 0.5
            m_new = jnp.maximum(ms[h], jnp.where(chosen, mx_ref[h], NEG_BIG))
            alpha = jnp.exp(ms[h] - m_new)
            p = jnp.exp(s_ref[h] - jnp.where(chosen, m_new, -NEG_BIG))
            new_ms.append(m_new)
            new_ls.append(alpha * ls[h] + jnp.sum(p, axis=0, keepdims=True))
            rows = slice(h * MB_HD, (h + 1) * MB_HD)
            acc_ref[rows, :] = alpha * acc_ref[rows, :] + _dot(vtn, _bf16(p))
        return tuple(new_ms), tuple(new_ls)

    def pair(j, carry):
        n = 2 * j
        scores_into(sb_ref, mxb_ref, n + 1)
        carry = attend(sa_ref, mxa_ref, n, carry)
        scores_into(sa_ref, mxa_ref, jnp.minimum(n + 2, nb - 1))
        return attend(sb_ref, mxb_ref, n + 1, carry)

    carry = lax.fori_loop(0, i // 2, pair, (tuple(ms), tuple(ls)))
    ms, ls = lax.cond(i % 2 == 1, lambda c: attend(sa_ref, mxa_ref, i - 1, c), lambda c: c, carry)

    for h in heads:
        rows = slice(h * MB_HD, (h + 1) * MB_HD)
        acc_ref[rows, :] = acc_ref[rows, :] / ls[h]
    o_ref[...] = _bf16(acc_ref[...].T)


def _moba(q, k, vt, km, batch, seq):
    nb = seq // MB_BLOCK
    t = batch * seq
    gw = MB_GROUP * MB_HD
    return pl.pallas_call(
        _moba_kernel,
        grid=(batch, MB_KV_HEADS, nb),
        in_specs=[pl.BlockSpec((1, MB_GROUP, MB_BLOCK, MB_HD), lambda b, g, i: (b, g, i, 0)),
                  pl.BlockSpec((1, 1, nb, MB_BLOCK, MB_HD), lambda b, g, i: (b, g, 0, 0, 0)),
                  pl.BlockSpec((1, 1, nb, MB_HD, MB_BLOCK), lambda b, g, i: (b, g, 0, 0, 0)),
                  pl.BlockSpec((1, 1, nb, MB_HD), lambda b, g, i: (b, g, 0, 0))],
        out_specs=pl.BlockSpec((MB_BLOCK, gw), lambda b, g, i: (b * nb + i, g)),
        out_shape=jax.ShapeDtypeStruct((t, MB_HEADS * MB_HD), jnp.bfloat16),
        scratch_shapes=[pltpu.VMEM((MB_GROUP, nb, MB_BLOCK), jnp.float32),
                        pltpu.VMEM((MB_GROUP, MB_BLOCK, MB_BLOCK), jnp.float32),
                        pltpu.VMEM((MB_GROUP, MB_BLOCK, MB_BLOCK), jnp.float32),
                        pltpu.VMEM((MB_GROUP, 1, MB_BLOCK), jnp.float32),
                        pltpu.VMEM((MB_GROUP, 1, MB_BLOCK), jnp.float32),
                        pltpu.VMEM((gw, MB_BLOCK), jnp.float32)],
        compiler_params=_params(("parallel", "parallel", "arbitrary")),
        name="moba",
    )(q, k, vt, km)


def _rope_tables(seq):
    inv = ROPE_THETA ** (-jnp.arange(0, ROPE_DIM, 2, dtype=jnp.float32) / ROPE_DIM)
    ang = jnp.arange(seq, dtype=jnp.float32)[:, None] * inv[None, :]
    cos, sin = jnp.cos(ang), jnp.sin(ang)
    zeros = jnp.zeros((seq, MB_HD - ROPE_DIM), jnp.float32)
    zh = jnp.zeros((seq, ROPE_HALF), jnp.float32)
    cos_h = jnp.concatenate([cos, cos, jnp.ones_like(zeros)], axis=1)
    lo_h = jnp.concatenate([-sin, zh, zeros], axis=1)
    hi_h = jnp.concatenate([zh, sin, zeros], axis=1)
    two = lambda a: jnp.concatenate([a, a], axis=1)
    return two(cos_h), two(lo_h), two(hi_h)


def kernel(x, p, norm_mix, a_w_in, a_b_gate, a_mh_gain, a_w_out, kv_norm, w_kv, k_norm, b_w_q, b_q_norm, b_w_o,
           norm_ffn, w_gate_up, w_down, norm_ple, w_ple_gate, w_ple_up):
    batch, seq, d = x.shape
    t = batch * seq
    f32 = jnp.float32
    x2 = x.reshape(t, d)
    p2 = p.reshape(p.shape[0], t, PLE_DIM)
    row = lambda a: a.reshape(1, -1).astype(f32)

    w_in = a_w_in[0]
    n_main = 2 * ML_QK_COLS + 2 * ML_V_COLS
    w_main = _bf16(w_in[:, :n_main])
    w_gate = _bf16(jnp.pad(w_in[:, n_main:], ((0, 0), (0, GATE_LANES - 2 * ML_HEADS))))
    b_gate = jnp.pad(a_b_gate[0].astype(f32), (0, GATE_LANES - 2 * ML_HEADS)).reshape(1, GATE_LANES)
    q, k, v, o, gcol = _inproj(x2, row(norm_mix[0]), w_main, w_gate, b_gate)
    grow = jnp.transpose(gcol.reshape(batch, seq, GATE_LANES)[:, :, :3 * ML_HEADS], (0, 2, 1))
    y = _mlstm(q, k, v, o, gcol, grow, row(a_mh_gain[0]), batch, seq)
    h = _post(x2, y, p2[0], _bf16(a_w_out[0]), row(norm_ffn[0]), _bf16(w_gate_up[0]), _bf16(w_down[0]),
              row(norm_ple[0]), _bf16(w_ple_gate[0]), _bf16(w_ple_up[0]))

    nb = seq // MB_BLOCK
    kvw = MB_KV_HEADS * MB_HD
    cos_t, sin_lo, sin_hi = _rope_tables(seq)
    head = lax.broadcasted_iota(jnp.int32, (kvw, kvw), 0) // MB_HD
    ones_bd = (head == head.T).astype(jnp.bfloat16)
    qn = row(jnp.tile(b_q_norm[0], MB_HEADS))
    kn = row(jnp.tile(k_norm, MB_KV_HEADS))
    q, k, v, km = _proj(h, row(norm_mix[1]), row(kv_norm), _bf16(b_w_q[0]), _bf16(w_kv), qn, kn, ones_bd,
                        cos_t, sin_lo, sin_hi, seq)
    qh = jnp.transpose(q.reshape(batch, seq, MB_HEADS, MB_HD), (0, 2, 1, 3))
    kh = jnp.transpose(k.reshape(batch, nb, MB_BLOCK, MB_KV_HEADS, MB_HD), (0, 3, 1, 2, 4))
    vth = jnp.transpose(v.reshape(batch, nb, MB_BLOCK, MB_KV_HEADS, MB_HD), (0, 3, 1, 4, 2))
    kmh = jnp.transpose(km.reshape(batch, nb, MB_KV_HEADS, MB_HD), (0, 2, 1, 3))
    y = _moba(qh, kh, vth, kmh, batch, seq)
    h = _post(h, y, p2[1], _bf16(b_w_o[0]), row(norm_ffn[1]), _bf16(w_gate_up[1]), _bf16(w_down[1]),
              row(norm_ple[1]), _bf16(w_ple_gate[1]), _bf16(w_ple_up[1]))
    return h.reshape(batch, seq, d)
```

```python
import jax
import jax.numpy as jnp
from jax import lax
from jax.experimental import pallas as pl
from jax.experimental.pallas import tpu as pltpu

D_MODEL = 1024
PLE_DIM = 256
EPS = 1e-6

ML_HEADS = 8
ML_DV = 128
ML_DQK = 64
GATE_CAP = 15.0
ML_QK_COLS = ML_HEADS * ML_DQK
ML_V_COLS = ML_HEADS * ML_DV
ML_CHUNK = 256

MB_HEADS = 16
MB_KV_HEADS = 4
MB_HD = 64
MB_GROUP = MB_HEADS // MB_KV_HEADS
MB_BLOCK = 256
MB_TOPK = 3
ROPE_THETA = 500000.0
ROPE_DIM = MB_HD // 4
ROPE_HALF = ROPE_DIM // 2

FFN_HIDDEN = 2816

LANES = 128
GATE_LANES = LANES
NEG_BIG = -1e30
VMEM_LIMIT = 56 * 1024 * 1024

ROW_TILE = 512
POST_ROW_TILE = 256


def _bf16(x):
    return x.astype(jnp.bfloat16)


def _dot(a, b):
    return jnp.dot(a, b, preferred_element_type=jnp.float32)


def _dot_nt(a, b):
    return lax.dot_general(a, b, (((1,), (1,)), ((), ())), preferred_element_type=jnp.float32)


def _dot_tn(a, b):
    return lax.dot_general(a, b, (((0,), (0,)), ((), ())), preferred_element_type=jnp.float32)


def _rms_rows(x):
    return x * lax.rsqrt(jnp.mean(x * x, axis=-1, keepdims=True) + EPS)


def _const_spec(shape):
    nd = len(shape)
    return pl.BlockSpec(shape, lambda *_: (0,) * nd, pipeline_mode=pl.Buffered(1))


def _params(sem):
    return pltpu.CompilerParams(dimension_semantics=sem, vmem_limit_bytes=VMEM_LIMIT)


def _inproj_kernel(x_ref, g_ref, w_ref, wg_ref, bg_ref, q_ref, k_ref, v_ref, o_ref, gate_ref):
    hn = _bf16(_rms_rows(x_ref[...]) * g_ref[...])
    c0, c1, c2, c3 = ML_QK_COLS, 2 * ML_QK_COLS, 2 * ML_QK_COLS + ML_V_COLS, 2 * ML_QK_COLS + 2 * ML_V_COLS
    q_ref[...] = _bf16(_dot(hn, w_ref[:, 0:c0]) * (ML_DQK ** -0.5))
    k_ref[...] = _bf16(_dot(hn, w_ref[:, c0:c1]))
    v_ref[...] = _bf16(_dot(hn, w_ref[:, c1:c2]))
    o_ref[...] = _bf16(_dot(hn, w_ref[:, c2:c3]))

    pre = _dot(hn, wg_ref[...]) + bg_ref[...]
    cap = GATE_CAP * jnp.tanh(pre * (1.0 / GATE_CAP))
    logf = jnp.minimum(cap, 0.0) - jnp.log1p(jnp.exp(-jnp.abs(cap)))
    rows = cap.shape[0]
    row_in_chunk = lax.broadcasted_iota(jnp.int32, cap.shape, 0) % ML_CHUNK
    cum = logf
    d = 1
    while d < ML_CHUNK:
        cum = cum + jnp.where(row_in_chunk >= d, pltpu.roll(cum, d, 0), 0.0)
        d *= 2
    li_shift = pltpu.roll(cap, ML_HEADS, 1)
    parts = []
    for c in range(rows // ML_CHUNK):
        sl = slice(c * ML_CHUNK, (c + 1) * ML_CHUNK)
        g_last = cum[(c + 1) * ML_CHUNK - 1:(c + 1) * ML_CHUNK, :]
        parts.append(g_last - cum[sl] + li_shift[sl])
    wlog = pltpu.roll(jnp.concatenate(parts, axis=0), ML_HEADS, 1)
    lane = lax.broadcasted_iota(jnp.int32, cap.shape, 1)
    gate_ref[...] = jnp.where(lane < ML_HEADS, cap, jnp.where(lane < 2 * ML_HEADS, cum, wlog))


def _inproj(x2, g, w, wg, bg):
    t = x2.shape[0]
    tm = ROW_TILE
    row = lambda n: pl.BlockSpec((tm, n), lambda i: (i, 0))
    return pl.pallas_call(
        _inproj_kernel,
        grid=(t // tm,),
        in_specs=[row(D_MODEL), _const_spec((1, D_MODEL)), _const_spec(w.shape), _const_spec(wg.shape),
                  _const_spec((1, GATE_LANES))],
        out_specs=[row(ML_QK_COLS), row(ML_QK_COLS), row(ML_V_COLS), row(ML_V_COLS), row(GATE_LANES)],
        out_shape=[jax.ShapeDtypeStruct((t, ML_QK_COLS), jnp.bfloat16),
                   jax.ShapeDtypeStruct((t, ML_QK_COLS), jnp.bfloat16),
                   jax.ShapeDtypeStruct((t, ML_V_COLS), jnp.bfloat16),
                   jax.ShapeDtypeStruct((t, ML_V_COLS), jnp.bfloat16),
                   jax.ShapeDtypeStruct((t, GATE_LANES), jnp.float32)],
        compiler_params=_params(("parallel",)),
        name="inproj",
    )(x2, g, w, wg, bg)


def _mlstm_kernel(q_ref, k_ref, v_ref, o_ref, gcol_ref, grow_ref, gain_ref, y_ref, c_ref):
    L = ML_CHUNK

    @pl.when(pl.program_id(1) == 0)
    def _():
        c_ref[...] = jnp.zeros_like(c_ref)

    lane_row = lax.broadcasted_iota(jnp.int32, (1, LANES), 1)
    r_i = lax.broadcasted_iota(jnp.int32, (L, L), 0)
    c_i = lax.broadcasted_iota(jnp.int32, (L, L), 1)
    causal = r_i >= c_i
    gc = gcol_ref[...]
    egc = jnp.exp(gc)
    gain = gain_ref[...]
    heads = range(ML_HEADS)
    pair = lambda h: slice((h // 2) * LANES, (h // 2 + 1) * LANES)
    cols = lambda h: slice(h * ML_DV, (h + 1) * ML_DV)
    col = lambda a, c: a[:, c:c + 1]

    ones = jnp.ones((L, ML_DV), jnp.bfloat16)
    vs = [jnp.concatenate([v_ref[:, cols(h)], ones], axis=1) for h in heads]

    scores, inter = [], []
    for h in heads:
        head_lanes = (lane_row // ML_DQK) == (h % 2)
        q = jnp.where(head_lanes, q_ref[:, pair(h)], jnp.zeros((), jnp.bfloat16))
        scores.append(_dot_nt(q, k_ref[:, pair(h)]))
        inter.append(_dot(q, _bf16(c_ref[h])))

    ps = []
    for h in heads:
        c_row = grow_ref[0, h:h + 1, :] - grow_ref[0, ML_HEADS + h:ML_HEADS + h + 1, :]
        dmat = jnp.where(causal, col(gc, ML_HEADS + h) + c_row, NEG_BIG)
        ps.append(_bf16(scores[h] * jnp.exp(dmat)))

    for h in heads:
        both = _dot(ps[h], vs[h]) + inter[h] * col(egc, ML_HEADS + h)
        hc = both[:, :ML_DV] / jnp.maximum(jnp.abs(both[:, ML_DV:]), 1.0)
        hn = _rms_rows(hc) * gain
        y_ref[:, cols(h)] = _bf16(jax.nn.sigmoid(o_ref[:, cols(h)].astype(jnp.float32)) * hn)

    for h in heads:
        decay = jnp.exp(grow_ref[0, ML_HEADS + h:ML_HEADS + h + 1, L - 1:L])
        kw = k_ref[:, pair(h)].astype(jnp.float32) * col(egc, 2 * ML_HEADS + h)
        c_ref[h] = decay * c_ref[h] + _dot_tn(_bf16(kw), vs[h])


def _mlstm(q, k, v, o, gcol, grow, gain, batch, seq):
    L = ML_CHUNK
    nc = seq // L
    t = batch * seq
    row = lambda n: pl.BlockSpec((L, n), lambda b, c: (b * nc + c, 0))
    return pl.pallas_call(
        _mlstm_kernel,
        grid=(batch, nc),
        in_specs=[row(ML_QK_COLS), row(ML_QK_COLS), row(ML_V_COLS), row(ML_V_COLS), row(GATE_LANES),
                  pl.BlockSpec((1, 3 * ML_HEADS, L), lambda b, c: (b, 0, c)),
                  pl.BlockSpec((1, ML_DV), lambda b, c: (0, 0))],
        out_specs=row(ML_V_COLS),
        out_shape=jax.ShapeDtypeStruct((t, ML_V_COLS), jnp.bfloat16),
        scratch_shapes=[pltpu.VMEM((ML_HEADS, LANES, 2 * ML_DV), jnp.float32)],
        compiler_params=_params(("parallel", "arbitrary")),
        name="mlstm",
    )(q, k, v, o, gcol, grow, gain)


def _post_kernel(h_ref, y_ref, p_ref, wo_ref, gf_ref, wgu_ref, wd_ref, gp_ref, wpg_ref, wpu_ref, out_ref):
    h1 = h_ref[...] + _dot(y_ref[...], wo_ref[...])
    hn = _bf16(_rms_rows(h1) * gf_ref[...])
    gu = _dot(hn, wgu_ref[...])
    act = _bf16(jax.nn.silu(gu[:, :FFN_HIDDEN]) * gu[:, FFN_HIDDEN:])
    h2 = h1 + _dot(act, wd_ref[...])
    hp = _bf16(_rms_rows(h2) * gp_ref[...])
    gate = jax.nn.sigmoid(_dot(hp, wpg_ref[...]))
    up = _dot(_bf16(p_ref[...]), wpu_ref[...])
    out_ref[...] = h2 + up * gate


def _post(h, y, p, layer, wo, gf, wgu, wd, gp, wpg, wpu):
    t = h.shape[0]
    tm = POST_ROW_TILE
    row = lambda n: pl.BlockSpec((tm, n), lambda i: (i, 0))
    p_spec = pl.BlockSpec((None, tm, PLE_DIM), lambda i: (layer, i, 0))
    return pl.pallas_call(
        _post_kernel,
        grid=(t // tm,),
        in_specs=[row(D_MODEL), row(D_MODEL), p_spec, _const_spec(wo.shape), _const_spec((1, D_MODEL)),
                  _const_spec(wgu.shape), _const_spec(wd.shape), _const_spec((1, D_MODEL)),
                  _const_spec(wpg.shape), _const_spec(wpu.shape)],
        out_specs=row(D_MODEL),
        out_shape=jax.ShapeDtypeStruct((t, D_MODEL), jnp.float32),
        compiler_params=_params(("parallel",)),
        name="post",
    )(h, y, p, wo, gf, wgu, wd, gp, wpg, wpu)


def _head_norm_rope(x, gain, ones_bd, cos_t, sin_lo, sin_hi):
    sq = x * x
    hi = _bf16(sq)
    lo = _bf16(sq - hi.astype(jnp.float32))
    ms = (_dot(hi, ones_bd) + _dot(lo, ones_bd)) * (1.0 / MB_HD)
    xn = x * lax.rsqrt(ms + EPS) * gain
    outs = []
    for s in range(x.shape[1] // LANES):
        xs = xn[:, s * LANES:(s + 1) * LANES]
        outs.append(xs * cos_t + pltpu.roll(xs, LANES - ROPE_HALF, 1) * sin_lo + pltpu.roll(xs, ROPE_HALF, 1) * sin_hi)
    return jnp.concatenate(outs, axis=1)


def _proj_kernel(h_ref, gq_ref, gkv_ref, wq_ref, wkv_ref, qn_ref, kn_ref, bd_ref, cos_ref, slo_ref, shi_ref,
                 q_ref, k_ref, vt_ref, km_ref):
    r = _rms_rows(h_ref[...])
    hq = _bf16(r * gq_ref[...])
    hkv = _bf16(r * gkv_ref[...])
    cos_t, sin_lo, sin_hi = cos_ref[...], slo_ref[...], shi_ref[...]
    bd = bd_ref[...]
    kvw = MB_KV_HEADS * MB_HD
    for s in range(D_MODEL // kvw):
        cols = slice(s * kvw, (s + 1) * kvw)
        qs = _dot(hq, wq_ref[:, cols])
        qs = _head_norm_rope(qs, qn_ref[:, cols], bd, cos_t, sin_lo, sin_hi)
        q_ref[:, cols] = _bf16(qs * (MB_HD ** -0.5))
    ks = _head_norm_rope(_dot(hkv, wkv_ref[:, :kvw]), kn_ref[...], bd, cos_t, sin_lo, sin_hi)
    k_ref[...] = _bf16(ks)
    vs = _dot(hkv, wkv_ref[:, kvw:])
    for b in range(ks.shape[0] // MB_BLOCK):
        rows = slice(b * MB_BLOCK, (b + 1) * MB_BLOCK)
        vt_ref[b] = _bf16(vs[rows].T)
        km_ref[b] = jnp.mean(ks[rows], axis=0, keepdims=True)


def _proj(h, gq, gkv, wq, wkv, qn, kn, bd, cos_t, sin_lo, sin_hi, seq):
    t = h.shape[0]
    tm = ROW_TILE
    kvw = MB_KV_HEADS * MB_HD
    row = lambda n: pl.BlockSpec((tm, n), lambda i: (i, 0))
    tab = pl.BlockSpec((tm, LANES), lambda i: (i % (seq // tm), 0))
    return pl.pallas_call(
        _proj_kernel,
        grid=(t // tm,),
        in_specs=[row(D_MODEL), _const_spec((1, D_MODEL)), _const_spec((1, D_MODEL)), _const_spec(wq.shape),
                  _const_spec(wkv.shape), _const_spec((1, D_MODEL)), _const_spec((1, kvw)), _const_spec(bd.shape),
                  tab, tab, tab],
        out_specs=[row(D_MODEL), row(kvw),
                   pl.BlockSpec((tm // MB_BLOCK, kvw, MB_BLOCK), lambda i: (i, 0, 0)),
                   pl.BlockSpec((tm // MB_BLOCK, 1, kvw), lambda i: (i, 0, 0))],
        out_shape=[jax.ShapeDtypeStruct((t, D_MODEL), jnp.bfloat16),
                   jax.ShapeDtypeStruct((t, kvw), jnp.bfloat16),
                   jax.ShapeDtypeStruct((t // MB_BLOCK, kvw, MB_BLOCK), jnp.bfloat16),
                   jax.ShapeDtypeStruct((t // MB_BLOCK, 1, kvw), jnp.float32)],
        compiler_params=_params(("parallel",)),
        name="proj",
    )(h, gq, gkv, wq, wkv, qn, kn, bd, cos_t, sin_lo, sin_hi)


def _moba_kernel(q_ref, k_ref, vt_ref, km_ref, o_ref, qa_ref, sel_ref, sa_ref, sb_ref, mxa_ref, mxb_ref, acc_ref):
    i = pl.program_id(2)
    nb = km_ref.shape[0]
    heads = range(MB_GROUP)
    half = pl.program_id(1) % 2
    lane_row = lax.broadcasted_iota(jnp.int32, (1, LANES), 1)
    own_half = (lane_row // MB_HD) == half
    for u in range(MB_GROUP // 2):
        t = q_ref[:, u * LANES:(u + 1) * LANES].astype(jnp.float32)
        tr = pltpu.roll(t, MB_HD, 1)
        qa_ref[2 * u] = _bf16(jnp.where(own_half, jnp.where(half == 0, t, tr), 0.0))
        qa_ref[2 * u + 1] = _bf16(jnp.where(own_half, jnp.where(half == 0, tr, t), 0.0))
    km = jnp.where(own_half, km_ref[:, 0, :], 0.0)
    km_hi = _bf16(km)
    km_lo = _bf16(km - km_hi.astype(jnp.float32))
    blk = lax.broadcasted_iota(jnp.int32, (nb, MB_BLOCK), 0)
    past = blk < i
    key_i = lax.broadcasted_iota(jnp.int32, (MB_BLOCK, MB_BLOCK), 0)
    qry_i = lax.broadcasted_iota(jnp.int32, (MB_BLOCK, MB_BLOCK), 1)
    causal = key_i <= qry_i

    def scores_into(s_ref, mx_ref, n):
        kn = k_ref[n]
        for h in heads:
            s = _dot_nt(kn, qa_ref[h])
            s_ref[h] = s
            mx_ref[h] = jnp.max(s, axis=0, keepdims=True)

    qs = [qa_ref[h] for h in heads]
    gates = [jnp.where(past, _dot_nt(km_hi, q) + _dot_nt(km_lo, q), -jnp.inf) for q in qs]
    own = [jnp.where(causal, _dot_nt(k_ref[i], q), NEG_BIG) for q in qs]
    scores_into(sa_ref, mxa_ref, 0)
    for h in heads:
        rank = jnp.zeros((nb, MB_BLOCK), jnp.int32)
        for m in range(nb):
            gm = gates[h][m:m + 1, :]
            beats = (gm > gates[h]) | ((gm == gates[h]) & (m < blk))
            rank = rank + beats.astype(jnp.int32)
        sel_ref[h] = ((rank < MB_TOPK) & past).astype(jnp.float32)

    vt_own = vt_ref[i]
    ms, ls = [], []
    for h in heads:
        mx = jnp.max(own[h], axis=0, keepdims=True)
        p = jnp.exp(own[h] - mx)
        ms.append(mx)
        ls.append(jnp.sum(p, axis=0, keepdims=True))
        acc_ref[h * MB_HD:(h + 1) * MB_HD, :] = _dot(vt_own, _bf16(p))

    def attend(s_ref, mx_ref, n, carry):
        ms, ls = carry
        vtn = vt_ref[n]
        new_ms, new_ls = [], []
        for h in heads:
            chosen = sel_ref[h, pl.ds(n, 1), :] > 0.5
            m_new = jnp.maximum(ms[h], jnp.where(chosen, mx_ref[h], NEG_BIG))
            alpha = jnp.exp(ms[h] - m_new)
            p = jnp.exp(s_ref[h] - jnp.where(chosen, m_new, -NEG_BIG))
            new_ms.append(m_new)
            new_ls.append(alpha * ls[h] + jnp.sum(p, axis=0, keepdims=True))
            rows = slice(h * MB_HD, (h + 1) * MB_HD)
            acc_ref[rows, :] = alpha * acc_ref[rows, :] + _dot(vtn, _bf16(p))
        return tuple(new_ms), tuple(new_ls)

    def pair(j, carry):
        n = 2 * j
        scores_into(sb_ref, mxb_ref, n + 1)
        carry = attend(sa_ref, mxa_ref, n, carry)
        scores_into(sa_ref, mxa_ref, jnp.minimum(n + 2, nb - 1))
        return attend(sb_ref, mxb_ref, n + 1, carry)

    carry = lax.fori_loop(0, i // 2, pair, (tuple(ms), tuple(ls)))
    ms, ls = lax.cond(i % 2 == 1, lambda c: attend(sa_ref, mxa_ref, i - 1, c), lambda c: c, carry)

    for h in heads:
        rows = slice(h * MB_HD, (h + 1) * MB_HD)
        acc_ref[rows, :] = acc_ref[rows, :] / ls[h]
    o_ref[...] = _bf16(acc_ref[...].T)


def _moba(q, k, vt, km, batch, seq):
    nb = seq // MB_BLOCK
    t = batch * seq
    gw = MB_GROUP * MB_HD
    return pl.pallas_call(
        _moba_kernel,
        grid=(batch, MB_KV_HEADS, nb),
        in_specs=[pl.BlockSpec((MB_BLOCK, gw), lambda b, g, i: (b * nb + i, g)),
                  pl.BlockSpec((nb, MB_BLOCK, LANES), lambda b, g, i: (b, 0, g // 2)),
                  pl.BlockSpec((nb, MB_HD, MB_BLOCK), lambda b, g, i: (b, g, 0)),
                  pl.BlockSpec((nb, 1, LANES), lambda b, g, i: (b, 0, g // 2))],
        out_specs=pl.BlockSpec((MB_BLOCK, gw), lambda b, g, i: (b * nb + i, g)),
        out_shape=jax.ShapeDtypeStruct((t, MB_HEADS * MB_HD), jnp.bfloat16),
        scratch_shapes=[pltpu.VMEM((MB_GROUP, MB_BLOCK, LANES), jnp.bfloat16),
                        pltpu.VMEM((MB_GROUP, nb, MB_BLOCK), jnp.float32),
                        pltpu.VMEM((MB_GROUP, MB_BLOCK, MB_BLOCK), jnp.float32),
                        pltpu.VMEM((MB_GROUP, MB_BLOCK, MB_BLOCK), jnp.float32),
                        pltpu.VMEM((MB_GROUP, 1, MB_BLOCK), jnp.float32),
                        pltpu.VMEM((MB_GROUP, 1, MB_BLOCK), jnp.float32),
                        pltpu.VMEM((gw, MB_BLOCK), jnp.float32)],
        compiler_params=_params(("parallel", "parallel", "arbitrary")),
        name="moba",
    )(q, k, vt, km)


def _rope_tables(seq):
    inv = ROPE_THETA ** (-jnp.arange(0, ROPE_DIM, 2, dtype=jnp.float32) / ROPE_DIM)
    ang = jnp.arange(seq, dtype=jnp.float32)[:, None] * inv[None, :]
    cos, sin = jnp.cos(ang), jnp.sin(ang)
    zeros = jnp.zeros((seq, MB_HD - ROPE_DIM), jnp.float32)
    zh = jnp.zeros((seq, ROPE_HALF), jnp.float32)
    cos_h = jnp.concatenate([cos, cos, jnp.ones_like(zeros)], axis=1)
    lo_h = jnp.concatenate([-sin, zh, zeros], axis=1)
    hi_h = jnp.concatenate([zh, sin, zeros], axis=1)
    two = lambda a: jnp.concatenate([a, a], axis=1)
    return two(cos_h), two(lo_h), two(hi_h)


def kernel(x, p, norm_mix, a_w_in, a_b_gate, a_mh_gain, a_w_out, kv_norm, w_kv, k_norm, b_w_q, b_q_norm, b_w_o,
           norm_ffn, w_gate_up, w_down, norm_ple, w_ple_gate, w_ple_up):
    batch, seq, d = x.shape
    t = batch * seq
    f32 = jnp.float32
    x2 = x.reshape(t, d)
    p2 = p.reshape(p.shape[0], t, PLE_DIM)
    row = lambda a: a.reshape(1, -1).astype(f32)

    w_in = a_w_in[0]
    n_main = 2 * ML_QK_COLS + 2 * ML_V_COLS
    w_main = _bf16(w_in[:, :n_main])
    w_gate = _bf16(jnp.pad(w_in[:, n_main:], ((0, 0), (0, GATE_LANES - 2 * ML_HEADS))))
    b_gate = jnp.pad(a_b_gate[0].astype(f32), (0, GATE_LANES - 2 * ML_HEADS)).reshape(1, GATE_LANES)
    q, k, v, o, gcol = _inproj(x2, row(norm_mix[0]), w_main, w_gate, b_gate)
    grow = jnp.transpose(gcol.reshape(batch, seq, GATE_LANES)[:, :, :3 * ML_HEADS], (0, 2, 1))
    y = _mlstm(q, k, v, o, gcol, grow, row(a_mh_gain[0]), batch, seq)
    h = _post(x2, y, p2, 0, _bf16(a_w_out[0]), row(norm_ffn[0]), _bf16(w_gate_up[0]), _bf16(w_down[0]),
              row(norm_ple[0]), _bf16(w_ple_gate[0]), _bf16(w_ple_up[0]))

    nb = seq // MB_BLOCK
    kvw = MB_KV_HEADS * MB_HD
    cos_t, sin_lo, sin_hi = _rope_tables(seq)
    head = lax.broadcasted_iota(jnp.int32, (kvw, kvw), 0) // MB_HD
    ones_bd = (head == head.T).astype(jnp.bfloat16)
    qn = row(jnp.tile(b_q_norm[0], MB_HEADS))
    kn = row(jnp.tile(k_norm, MB_KV_HEADS))
    q, k, vt, km = _proj(h, row(norm_mix[1]), row(kv_norm), _bf16(b_w_q[0]), _bf16(w_kv), qn, kn, ones_bd,
                         cos_t, sin_lo, sin_hi, seq)
    y = _moba(q, k.reshape(batch * nb, MB_BLOCK, kvw), vt, km, batch, seq)
    h = _post(h, y, p2, 1, _bf16(b_w_o[0]), row(norm_ffn[1]), _bf16(w_gate_up[1]), _bf16(w_down[1]),
              row(norm_ple[1]), _bf16(w_ple_gate[1]), _bf16(w_ple_up[1]))
    return h.reshape(batch, seq, d)
```

```python
import math

import jax
import jax.numpy as jnp
from jax import lax
from jax.experimental import pallas as pl
from jax.experimental.pallas import tpu as pltpu

D_MODEL = 1024
PLE_DIM = 256
EPS = 1e-6

ML_HEADS = 8
ML_DV = 128
ML_DQK = 64
GATE_CAP = 15.0
ML_QK_COLS = ML_HEADS * ML_DQK
ML_V_COLS = ML_HEADS * ML_DV
ML_CHUNK = 256

MB_HEADS = 16
MB_KV_HEADS = 4
MB_HD = 64
MB_GROUP = MB_HEADS // MB_KV_HEADS
MB_BLOCK = 256
MB_TOPK = 3
ROPE_THETA = 500000.0
ROPE_DIM = MB_HD // 4
ROPE_HALF = ROPE_DIM // 2
MB_Q_SCALE = MB_HD ** -0.5 * math.log2(math.e)

FFN_HIDDEN = 2816

LANES = 128
GATE_LANES = LANES
NEG_BIG = -1e30
VMEM_LIMIT = 56 * 1024 * 1024

ROW_TILE = 512
POST_ROW_TILE = 256


def _bf16(x):
    return x.astype(jnp.bfloat16)


def _dot(a, b):
    return jnp.dot(a, b, preferred_element_type=jnp.float32)


def _dot_nt(a, b):
    return lax.dot_general(a, b, (((1,), (1,)), ((), ())), preferred_element_type=jnp.float32)


def _dot_tn(a, b):
    return lax.dot_general(a, b, (((0,), (0,)), ((), ())), preferred_element_type=jnp.float32)


def _rms_rows(x):
    return x * lax.rsqrt(jnp.mean(x * x, axis=-1, keepdims=True) + EPS)


def _const_spec(shape):
    nd = len(shape)
    return pl.BlockSpec(shape, lambda *_: (0,) * nd, pipeline_mode=pl.Buffered(1))


def _params(sem):
    return pltpu.CompilerParams(dimension_semantics=sem, vmem_limit_bytes=VMEM_LIMIT)


def _inproj_kernel(x_ref, g_ref, w_ref, wg_ref, bg_ref, q_ref, k_ref, v_ref, o_ref, gate_ref):
    hn = _bf16(_rms_rows(x_ref[...]) * g_ref[...])
    c0, c1, c2, c3 = ML_QK_COLS, 2 * ML_QK_COLS, 2 * ML_QK_COLS + ML_V_COLS, 2 * ML_QK_COLS + 2 * ML_V_COLS
    q_ref[...] = _bf16(_dot(hn, w_ref[:, 0:c0]) * (ML_DQK ** -0.5))
    k_ref[...] = _bf16(_dot(hn, w_ref[:, c0:c1]))
    v_ref[...] = _bf16(_dot(hn, w_ref[:, c1:c2]))
    o_ref[...] = _bf16(_dot(hn, w_ref[:, c2:c3]))

    pre = _dot(hn, wg_ref[...]) + bg_ref[...]
    cap = GATE_CAP * jnp.tanh(pre * (1.0 / GATE_CAP))
    logf = jnp.minimum(cap, 0.0) - jnp.log1p(jnp.exp(-jnp.abs(cap)))
    rows = cap.shape[0]
    row_in_chunk = lax.broadcasted_iota(jnp.int32, cap.shape, 0) % ML_CHUNK
    cum = logf
    d = 1
    while d < ML_CHUNK:
        cum = cum + jnp.where(row_in_chunk >= d, pltpu.roll(cum, d, 0), 0.0)
        d *= 2
    li_shift = pltpu.roll(cap, ML_HEADS, 1)
    parts = []
    for c in range(rows // ML_CHUNK):
        sl = slice(c * ML_CHUNK, (c + 1) * ML_CHUNK)
        g_last = cum[(c + 1) * ML_CHUNK - 1:(c + 1) * ML_CHUNK, :]
        parts.append(g_last - cum[sl] + li_shift[sl])
    wlog = pltpu.roll(jnp.concatenate(parts, axis=0), ML_HEADS, 1)
    lane = lax.broadcasted_iota(jnp.int32, cap.shape, 1)
    gate_ref[...] = jnp.where(lane < ML_HEADS, cap, jnp.where(lane < 2 * ML_HEADS, cum, wlog))


def _inproj(x2, g, w, wg, bg):
    t = x2.shape[0]
    tm = ROW_TILE
    row = lambda n: pl.BlockSpec((tm, n), lambda i: (i, 0))
    return pl.pallas_call(
        _inproj_kernel,
        grid=(t // tm,),
        in_specs=[row(D_MODEL), _const_spec((1, D_MODEL)), _const_spec(w.shape), _const_spec(wg.shape),
                  _const_spec((1, GATE_LANES))],
        out_specs=[row(ML_QK_COLS), row(ML_QK_COLS), row(ML_V_COLS), row(ML_V_COLS), row(GATE_LANES)],
        out_shape=[jax.ShapeDtypeStruct((t, ML_QK_COLS), jnp.bfloat16),
                   jax.ShapeDtypeStruct((t, ML_QK_COLS), jnp.bfloat16),
                   jax.ShapeDtypeStruct((t, ML_V_COLS), jnp.bfloat16),
                   jax.ShapeDtypeStruct((t, ML_V_COLS), jnp.bfloat16),
                   jax.ShapeDtypeStruct((t, GATE_LANES), jnp.float32)],
        compiler_params=_params(("parallel",)),
        name="inproj",
    )(x2, g, w, wg, bg)


def _mlstm_kernel(q_ref, k_ref, v_ref, o_ref, gcol_ref, grow_ref, gain_ref, y_ref, c_ref):
    L = ML_CHUNK

    @pl.when(pl.program_id(1) == 0)
    def _():
        c_ref[...] = jnp.zeros_like(c_ref)

    lane_row = lax.broadcasted_iota(jnp.int32, (1, LANES), 1)
    r_i = lax.broadcasted_iota(jnp.int32, (L, L), 0)
    c_i = lax.broadcasted_iota(jnp.int32, (L, L), 1)
    causal = r_i >= c_i
    gc = gcol_ref[...]
    egc = jnp.exp(gc)
    gain = gain_ref[...]
    heads = range(ML_HEADS)
    pair = lambda h: slice((h // 2) * LANES, (h // 2 + 1) * LANES)
    cols = lambda h: slice(h * ML_DV, (h + 1) * ML_DV)
    col = lambda a, c: a[:, c:c + 1]

    ones = jnp.ones((L, ML_DV), jnp.bfloat16)
    vs = [jnp.concatenate([v_ref[:, cols(h)], ones], axis=1) for h in heads]

    scores, inter = [], []
    for h in heads:
        head_lanes = (lane_row // ML_DQK) == (h % 2)
        q = jnp.where(head_lanes, q_ref[:, pair(h)], jnp.zeros((), jnp.bfloat16))
        scores.append(_dot_nt(q, k_ref[:, pair(h)]))
        inter.append(_dot(q, _bf16(c_ref[h])))

    ps = []
    for h in heads:
        c_row = grow_ref[0, h:h + 1, :] - grow_ref[0, ML_HEADS + h:ML_HEADS + h + 1, :]
        dmat = jnp.where(causal, col(gc, ML_HEADS + h) + c_row, NEG_BIG)
        ps.append(_bf16(scores[h] * jnp.exp(dmat)))

    for h in heads:
        both = _dot(ps[h], vs[h]) + inter[h] * col(egc, ML_HEADS + h)
        hc = both[:, :ML_DV] / jnp.maximum(jnp.abs(both[:, ML_DV:]), 1.0)
        hn = _rms_rows(hc) * gain
        y_ref[:, cols(h)] = _bf16(jax.nn.sigmoid(o_ref[:, cols(h)].astype(jnp.float32)) * hn)

    for h in heads:
        decay = jnp.exp(grow_ref[0, ML_HEADS + h:ML_HEADS + h + 1, L - 1:L])
        kw = k_ref[:, pair(h)].astype(jnp.float32) * col(egc, 2 * ML_HEADS + h)
        c_ref[h] = decay * c_ref[h] + _dot_tn(_bf16(kw), vs[h])


def _mlstm(q, k, v, o, gcol, grow, gain, batch, seq):
    L = ML_CHUNK
    nc = seq // L
    t = batch * seq
    row = lambda n: pl.BlockSpec((L, n), lambda b, c: (b * nc + c, 0))
    return pl.pallas_call(
        _mlstm_kernel,
        grid=(batch, nc),
        in_specs=[row(ML_QK_COLS), row(ML_QK_COLS), row(ML_V_COLS), row(ML_V_COLS), row(GATE_LANES),
                  pl.BlockSpec((1, 3 * ML_HEADS, L), lambda b, c: (b, 0, c)),
                  pl.BlockSpec((1, ML_DV), lambda b, c: (0, 0))],
        out_specs=row(ML_V_COLS),
        out_shape=jax.ShapeDtypeStruct((t, ML_V_COLS), jnp.bfloat16),
        scratch_shapes=[pltpu.VMEM((ML_HEADS, LANES, 2 * ML_DV), jnp.float32)],
        compiler_params=_params(("parallel", "arbitrary")),
        name="mlstm",
    )(q, k, v, o, gcol, grow, gain)


def _post_kernel(h_ref, y_ref, p_ref, wo_ref, gf_ref, wgu_ref, wd_ref, gp_ref, wpg_ref, wpu_ref, out_ref):
    h1 = h_ref[...] + _dot(y_ref[...], wo_ref[...])
    hn = _bf16(_rms_rows(h1) * gf_ref[...])
    gu = _dot(hn, wgu_ref[...])
    act = _bf16(jax.nn.silu(gu[:, :FFN_HIDDEN]) * gu[:, FFN_HIDDEN:])
    h2 = h1 + _dot(act, wd_ref[...])
    hp = _bf16(_rms_rows(h2) * gp_ref[...])
    gate = jax.nn.sigmoid(_dot(hp, wpg_ref[...]))
    up = _dot(_bf16(p_ref[...]), wpu_ref[...])
    out_ref[...] = h2 + up * gate


def _post(h, y, p, layer, wo, gf, wgu, wd, gp, wpg, wpu):
    t = h.shape[0]
    tm = POST_ROW_TILE
    row = lambda n: pl.BlockSpec((tm, n), lambda i: (i, 0))
    p_spec = pl.BlockSpec((None, tm, PLE_DIM), lambda i: (layer, i, 0))
    return pl.pallas_call(
        _post_kernel,
        grid=(t // tm,),
        in_specs=[row(D_MODEL), row(D_MODEL), p_spec, _const_spec(wo.shape), _const_spec((1, D_MODEL)),
                  _const_spec(wgu.shape), _const_spec(wd.shape), _const_spec((1, D_MODEL)),
                  _const_spec(wpg.shape), _const_spec(wpu.shape)],
        out_specs=row(D_MODEL),
        out_shape=jax.ShapeDtypeStruct((t, D_MODEL), jnp.float32),
        compiler_params=_params(("parallel",)),
        name="post",
    )(h, y, p, wo, gf, wgu, wd, gp, wpg, wpu)


def _head_norm_rope(x, gain, ones_bd, cos_t, sin_lo, sin_hi):
    sq = x * x
    hi = _bf16(sq)
    lo = _bf16(sq - hi.astype(jnp.float32))
    ms = (_dot(hi, ones_bd) + _dot(lo, ones_bd)) * (1.0 / MB_HD)
    xn = x * lax.rsqrt(ms + EPS) * gain
    outs = []
    for s in range(x.shape[1] // LANES):
        xs = xn[:, s * LANES:(s + 1) * LANES]
        outs.append(xs * cos_t + pltpu.roll(xs, LANES - ROPE_HALF, 1) * sin_lo + pltpu.roll(xs, ROPE_HALF, 1) * sin_hi)
    return jnp.concatenate(outs, axis=1)


def _proj_kernel(h_ref, gq_ref, gkv_ref, wq_ref, wk_ref, wv_ref, qn_ref, kn_ref, bd_ref, cos_ref, slo_ref, shi_ref,
                 q_ref, k_ref, vt_ref, km_ref):
    r = _rms_rows(h_ref[...])
    hq = _bf16(r * gq_ref[...])
    hkv = _bf16(r * gkv_ref[...])
    cos_t, sin_lo, sin_hi = cos_ref[...], slo_ref[...], shi_ref[...]
    bd = bd_ref[...]
    kvw = MB_KV_HEADS * MB_HD
    for s in range(D_MODEL // kvw):
        cols = slice(s * kvw, (s + 1) * kvw)
        qs = _dot(hq, wq_ref[:, cols])
        qs = _head_norm_rope(qs, qn_ref[:, cols], bd, cos_t, sin_lo, sin_hi)
        q_ref[:, cols] = _bf16(qs * MB_Q_SCALE)
    for s in range(2):
        cols = slice(s * kvw, (s + 1) * kvw)
        ks = _head_norm_rope(_dot(hkv, wk_ref[:, cols]), kn_ref[:, cols], bd, cos_t, sin_lo, sin_hi)
        k_ref[:, cols] = _bf16(ks)
        for b in range(ks.shape[0] // MB_BLOCK):
            km_ref[b, :, cols] = jnp.mean(ks[b * MB_BLOCK:(b + 1) * MB_BLOCK], axis=0, keepdims=True)
    vs = _dot(hkv, wv_ref[...])
    for b in range(vs.shape[0] // MB_BLOCK):
        vt_ref[b] = _bf16(vs[b * MB_BLOCK:(b + 1) * MB_BLOCK].T)


def _proj(h, gq, gkv, wq, wk2, wv, qn, kn2, bd, cos_t, sin_lo, sin_hi, seq):
    t = h.shape[0]
    tm = ROW_TILE
    kvw = MB_KV_HEADS * MB_HD
    row = lambda n: pl.BlockSpec((tm, n), lambda i: (i, 0))
    tab = pl.BlockSpec((tm, LANES), lambda i: (i % (seq // tm), 0))
    return pl.pallas_call(
        _proj_kernel,
        grid=(t // tm,),
        in_specs=[row(D_MODEL), _const_spec((1, D_MODEL)), _const_spec((1, D_MODEL)), _const_spec(wq.shape),
                  _const_spec(wk2.shape), _const_spec(wv.shape), _const_spec((1, D_MODEL)), _const_spec((1, 2 * kvw)),
                  _const_spec(bd.shape), tab, tab, tab],
        out_specs=[row(D_MODEL), row(2 * kvw),
                   pl.BlockSpec((tm // MB_BLOCK, kvw, MB_BLOCK), lambda i: (i, 0, 0)),
                   pl.BlockSpec((tm // MB_BLOCK, 1, 2 * kvw), lambda i: (i, 0, 0))],
        out_shape=[jax.ShapeDtypeStruct((t, D_MODEL), jnp.bfloat16),
                   jax.ShapeDtypeStruct((t, 2 * kvw), jnp.bfloat16),
                   jax.ShapeDtypeStruct((t // MB_BLOCK, kvw, MB_BLOCK), jnp.bfloat16),
                   jax.ShapeDtypeStruct((t // MB_BLOCK, 1, 2 * kvw), jnp.float32)],
        compiler_params=_params(("parallel",)),
        name="proj",
    )(h, gq, gkv, wq, wk2, wv, qn, kn2, bd, cos_t, sin_lo, sin_hi)


def _moba_kernel(q_ref, k_ref, vt_ref, km_ref, o_ref, qa_ref, sel_ref, sa_ref, sb_ref, mxa_ref, mxb_ref, acc_ref):
    i = pl.program_id(2)
    nb = km_ref.shape[0]
    heads = range(MB_GROUP)
    first_half = lax.broadcasted_iota(jnp.int32, (1, LANES), 1) < MB_HD
    zero = jnp.zeros((), jnp.bfloat16)
    for u in range(MB_GROUP // 2):
        t = q_ref[:, u * LANES:(u + 1) * LANES]
        qa_ref[2 * u] = jnp.where(first_half, t, zero)
        qa_ref[2 * u + 1] = jnp.where(first_half, zero, t)
    km = km_ref[:, 0, :]
    km_hi = _bf16(km)
    km_lo = _bf16(km - km_hi.astype(jnp.float32))
    blk = lax.broadcasted_iota(jnp.int32, (nb, MB_BLOCK), 0)
    past = blk < i
    key_i = lax.broadcasted_iota(jnp.int32, (MB_BLOCK, MB_BLOCK), 0)
    qry_i = lax.broadcasted_iota(jnp.int32, (MB_BLOCK, MB_BLOCK), 1)
    causal = key_i <= qry_i

    def scores_into(s_ref, mx_ref, n):
        kn = k_ref[n]
        for h in heads:
            s = _dot_nt(kn, qa_ref[h])
            s_ref[h] = s
            mx_ref[h] = jnp.max(s, axis=0, keepdims=True)

    own_and_means = jnp.concatenate([k_ref[i], km_hi, km_lo], axis=0)
    own, gates = [], []
    for h in heads:
        r = _dot_nt(own_and_means, qa_ref[h])
        own.append(jnp.where(causal, r[:MB_BLOCK], NEG_BIG))
        gates.append(jnp.where(past, r[MB_BLOCK:MB_BLOCK + nb] + r[MB_BLOCK + nb:], -jnp.inf))
    scores_into(sa_ref, mxa_ref, 0)
    for h in heads:
        g = gates[h]
        chosen = jnp.zeros((nb, MB_BLOCK), jnp.bool_)
        for _ in range(MB_TOPK):
            top = jnp.max(g, axis=0, keepdims=True)
            first = jnp.min(jnp.where(g == top, blk, nb), axis=0, keepdims=True)
            pick = blk == first
            chosen = chosen | pick
            g = jnp.where(pick, -jnp.inf, g)
        sel_ref[h] = (chosen & past).astype(jnp.float32)

    ones_rows = jnp.ones((16, MB_BLOCK), jnp.bfloat16)
    values = lambda n: jnp.concatenate([vt_ref[n], ones_rows], axis=0)

    def weights(s, offset):
        return jnp.exp2(_bf16(s - offset))

    vt_own = values(i)
    ms, ls = [], []
    for h in heads:
        mx = jnp.max(own[h], axis=0, keepdims=True)
        pv = _dot(vt_own, weights(own[h], mx))
        ms.append(mx)
        ls.append(pv[MB_HD:MB_HD + 1, :])
        acc_ref[h * MB_HD:(h + 1) * MB_HD, :] = pv[:MB_HD, :]

    def attend(s_ref, mx_ref, n, carry):
        ms, ls = carry
        vtn = values(n)
        new_ms, new_ls = [], []
        for h in heads:
            chosen = sel_ref[h, pl.ds(n, 1), :] > 0.5
            m_new = jnp.maximum(ms[h], jnp.where(chosen, mx_ref[h], NEG_BIG))
            alpha = jnp.exp2(ms[h] - m_new)
            pv = _dot(vtn, weights(s_ref[h], jnp.where(chosen, m_new, -NEG_BIG)))
            new_ms.append(m_new)
            new_ls.append(alpha * ls[h] + pv[MB_HD:MB_HD + 1, :])
            rows = slice(h * MB_HD, (h + 1) * MB_HD)
            acc_ref[rows, :] = alpha * acc_ref[rows, :] + pv[:MB_HD, :]
        return tuple(new_ms), tuple(new_ls)

    def pair(j, carry):
        n = 2 * j
        scores_into(sb_ref, mxb_ref, n + 1)
        carry = attend(sa_ref, mxa_ref, n, carry)
        scores_into(sa_ref, mxa_ref, jnp.minimum(n + 2, nb - 1))
        return attend(sb_ref, mxb_ref, n + 1, carry)

    carry = lax.fori_loop(0, i // 2, pair, (tuple(ms), tuple(ls)))
    ms, ls = lax.cond(i % 2 == 1, lambda c: attend(sa_ref, mxa_ref, i - 1, c), lambda c: c, carry)

    for h in heads:
        rows = slice(h * MB_HD, (h + 1) * MB_HD)
        acc_ref[rows, :] = acc_ref[rows, :] / ls[h]
    o_ref[...] = _bf16(acc_ref[...].T)


def _moba(q, k, vt, km, batch, seq):
    nb = seq // MB_BLOCK
    t = batch * seq
    gw = MB_GROUP * MB_HD
    return pl.pallas_call(
        _moba_kernel,
        grid=(batch, MB_KV_HEADS, nb),
        in_specs=[pl.BlockSpec((MB_BLOCK, gw), lambda b, g, i: (b * nb + i, g)),
                  pl.BlockSpec((nb, MB_BLOCK, LANES), lambda b, g, i: (b, 0, g)),
                  pl.BlockSpec((nb, MB_HD, MB_BLOCK), lambda b, g, i: (b, g, 0)),
                  pl.BlockSpec((nb, 1, LANES), lambda b, g, i: (b, 0, g))],
        out_specs=pl.BlockSpec((MB_BLOCK, gw), lambda b, g, i: (b * nb + i, g)),
        out_shape=jax.ShapeDtypeStruct((t, MB_HEADS * MB_HD), jnp.bfloat16),
        scratch_shapes=[pltpu.VMEM((MB_GROUP, MB_BLOCK, LANES), jnp.bfloat16),
                        pltpu.VMEM((MB_GROUP, nb, MB_BLOCK), jnp.float32),
                        pltpu.VMEM((MB_GROUP, MB_BLOCK, MB_BLOCK), jnp.float32),
                        pltpu.VMEM((MB_GROUP, MB_BLOCK, MB_BLOCK), jnp.float32),
                        pltpu.VMEM((MB_GROUP, 1, MB_BLOCK), jnp.float32),
                        pltpu.VMEM((MB_GROUP, 1, MB_BLOCK), jnp.float32),
                        pltpu.VMEM((gw, MB_BLOCK), jnp.float32)],
        compiler_params=_params(("parallel", "parallel", "arbitrary")),
        name="moba",
    )(q, k, vt, km)


def _rope_tables(seq):
    inv = ROPE_THETA ** (-jnp.arange(0, ROPE_DIM, 2, dtype=jnp.float32) / ROPE_DIM)
    ang = jnp.arange(seq, dtype=jnp.float32)[:, None] * inv[None, :]
    cos, sin = jnp.cos(ang), jnp.sin(ang)
    zeros = jnp.zeros((seq, MB_HD - ROPE_DIM), jnp.float32)
    zh = jnp.zeros((seq, ROPE_HALF), jnp.float32)
    cos_h = jnp.concatenate([cos, cos, jnp.ones_like(zeros)], axis=1)
    lo_h = jnp.concatenate([-sin, zh, zeros], axis=1)
    hi_h = jnp.concatenate([zh, sin, zeros], axis=1)
    two = lambda a: jnp.concatenate([a, a], axis=1)
    return two(cos_h), two(lo_h), two(hi_h)


def kernel(x, p, norm_mix, a_w_in, a_b_gate, a_mh_gain, a_w_out, kv_norm, w_kv, k_norm, b_w_q, b_q_norm, b_w_o,
           norm_ffn, w_gate_up, w_down, norm_ple, w_ple_gate, w_ple_up):
    batch, seq, d = x.shape
    t = batch * seq
    f32 = jnp.float32
    x2 = x.reshape(t, d)
    p2 = p.reshape(p.shape[0], t, PLE_DIM)
    row = lambda a: a.reshape(1, -1).astype(f32)

    w_in = a_w_in[0]
    n_main = 2 * ML_QK_COLS + 2 * ML_V_COLS
    w_main = _bf16(w_in[:, :n_main])
    w_gate = _bf16(jnp.pad(w_in[:, n_main:], ((0, 0), (0, GATE_LANES - 2 * ML_HEADS))))
    b_gate = jnp.pad(a_b_gate[0].astype(f32), (0, GATE_LANES - 2 * ML_HEADS)).reshape(1, GATE_LANES)
    q, k, v, o, gcol = _inproj(x2, row(norm_mix[0]), w_main, w_gate, b_gate)
    grow = jnp.transpose(gcol.reshape(batch, seq, GATE_LANES)[:, :, :3 * ML_HEADS], (0, 2, 1))
    y = _mlstm(q, k, v, o, gcol, grow, row(a_mh_gain[0]), batch, seq)
    h = _post(x2, y, p2, 0, _bf16(a_w_out[0]), row(norm_ffn[0]), _bf16(w_gate_up[0]), _bf16(w_down[0]),
              row(norm_ple[0]), _bf16(w_ple_gate[0]), _bf16(w_ple_up[0]))

    nb = seq // MB_BLOCK
    kvw = MB_KV_HEADS * MB_HD
    cos_t, sin_lo, sin_hi = _rope_tables(seq)
    head = lax.broadcasted_iota(jnp.int32, (kvw, kvw), 0) // MB_HD
    ones_bd = (head == head.T).astype(jnp.bfloat16)
    qn = row(jnp.tile(b_q_norm[0], MB_HEADS))
    kn2 = row(jnp.tile(k_norm, 2 * MB_KV_HEADS))
    w_k = _bf16(w_kv[:, :kvw]).reshape(d, MB_KV_HEADS, 1, MB_HD)
    w_k2 = jnp.broadcast_to(w_k, (d, MB_KV_HEADS, 2, MB_HD)).reshape(d, 2 * kvw)
    q, k2, vt, km2 = _proj(h, row(norm_mix[1]), row(kv_norm), _bf16(b_w_q[0]), w_k2, _bf16(w_kv[:, kvw:]), qn, kn2,
                           ones_bd, cos_t, sin_lo, sin_hi, seq)
    y = _moba(q, k2.reshape(batch * nb, MB_BLOCK, 2 * kvw), vt, km2, batch, seq)
    h = _post(h, y, p2, 1, _bf16(b_w_o[0]), row(norm_ffn[1]), _bf16(w_gate_up[1]), _bf16(w_down[1]),
              row(norm_ple[1]), _bf16(w_ple_gate[1]), _bf16(w_ple_up[1]))
    return h.reshape(batch, seq, d)
```

```python
import math

import jax
import jax.numpy as jnp
from jax import lax
from jax.experimental import pallas as pl
from jax.experimental.pallas import tpu as pltpu

D_MODEL = 1024
PLE_DIM = 256
EPS = 1e-6

ML_HEADS = 8
ML_DV = 128
ML_DQK = 64
GATE_CAP = 15.0
ML_QK_COLS = ML_HEADS * ML_DQK
ML_V_COLS = ML_HEADS * ML_DV
ML_CHUNK = 256

MB_HEADS = 16
MB_KV_HEADS = 4
MB_HD = 64
MB_GROUP = MB_HEADS // MB_KV_HEADS
MB_BLOCK = 256
MB_TOPK = 3
ROPE_THETA = 500000.0
ROPE_DIM = MB_HD // 4
ROPE_HALF = ROPE_DIM // 2
MB_Q_SCALE = MB_HD ** -0.5 * math.log2(math.e)

FFN_HIDDEN = 2816

LANES = 128
GATE_LANES = LANES
NEG_BIG = -1e30
VMEM_LIMIT = 56 * 1024 * 1024

ROW_TILE = 512
POST_ROW_TILE = 256


def _bf16(x):
    return x.astype(jnp.bfloat16)


def _dot(a, b):
    return jnp.dot(a, b, preferred_element_type=jnp.float32)


def _dot_nt(a, b):
    return lax.dot_general(a, b, (((1,), (1,)), ((), ())), preferred_element_type=jnp.float32)


def _dot_tn(a, b):
    return lax.dot_general(a, b, (((0,), (0,)), ((), ())), preferred_element_type=jnp.float32)


def _rms_rows(x):
    return x * lax.rsqrt(jnp.mean(x * x, axis=-1, keepdims=True) + EPS)


def _const_spec(shape):
    nd = len(shape)
    return pl.BlockSpec(shape, lambda *_: (0,) * nd, pipeline_mode=pl.Buffered(1))


def _params(sem):
    return pltpu.CompilerParams(dimension_semantics=sem, vmem_limit_bytes=VMEM_LIMIT)


def _inproj_kernel(x_ref, g_ref, w_ref, wg_ref, bg_ref, q_ref, k_ref, v_ref, o_ref, gate_ref):
    hn = _bf16(_rms_rows(x_ref[...]) * g_ref[...])
    c0, c1, c2, c3 = ML_QK_COLS, 2 * ML_QK_COLS, 2 * ML_QK_COLS + ML_V_COLS, 2 * ML_QK_COLS + 2 * ML_V_COLS
    q_ref[...] = _bf16(_dot(hn, w_ref[:, 0:c0]) * (ML_DQK ** -0.5))
    k_ref[...] = _bf16(_dot(hn, w_ref[:, c0:c1]))
    v_ref[...] = _bf16(_dot(hn, w_ref[:, c1:c2]))
    o_ref[...] = _bf16(_dot(hn, w_ref[:, c2:c3]))

    pre = _dot(hn, wg_ref[...]) + bg_ref[...]
    cap = GATE_CAP * jnp.tanh(pre * (1.0 / GATE_CAP))
    logf = jnp.minimum(cap, 0.0) - jnp.log1p(jnp.exp(-jnp.abs(cap)))
    rows = cap.shape[0]
    row_in_chunk = lax.broadcasted_iota(jnp.int32, cap.shape, 0) % ML_CHUNK
    cum = logf
    d = 1
    while d < ML_CHUNK:
        cum = cum + jnp.where(row_in_chunk >= d, pltpu.roll(cum, d, 0), 0.0)
        d *= 2
    li_shift = pltpu.roll(cap, ML_HEADS, 1)
    parts = []
    for c in range(rows // ML_CHUNK):
        sl = slice(c * ML_CHUNK, (c + 1) * ML_CHUNK)
        g_last = cum[(c + 1) * ML_CHUNK - 1:(c + 1) * ML_CHUNK, :]
        parts.append(g_last - cum[sl] + li_shift[sl])
    wlog = pltpu.roll(jnp.concatenate(parts, axis=0), ML_HEADS, 1)
    lane = lax.broadcasted_iota(jnp.int32, cap.shape, 1)
    gate_ref[...] = jnp.where(lane < ML_HEADS, cap, jnp.where(lane < 2 * ML_HEADS, cum, wlog))


def _inproj(x2, g, w, wg, bg):
    t = x2.shape[0]
    tm = ROW_TILE
    row = lambda n: pl.BlockSpec((tm, n), lambda i: (i, 0))
    return pl.pallas_call(
        _inproj_kernel,
        grid=(t // tm,),
        in_specs=[row(D_MODEL), _const_spec((1, D_MODEL)), _const_spec(w.shape), _const_spec(wg.shape),
                  _const_spec((1, GATE_LANES))],
        out_specs=[row(ML_QK_COLS), row(ML_QK_COLS), row(ML_V_COLS), row(ML_V_COLS), row(GATE_LANES)],
        out_shape=[jax.ShapeDtypeStruct((t, ML_QK_COLS), jnp.bfloat16),
                   jax.ShapeDtypeStruct((t, ML_QK_COLS), jnp.bfloat16),
                   jax.ShapeDtypeStruct((t, ML_V_COLS), jnp.bfloat16),
                   jax.ShapeDtypeStruct((t, ML_V_COLS), jnp.bfloat16),
                   jax.ShapeDtypeStruct((t, GATE_LANES), jnp.float32)],
        compiler_params=_params(("parallel",)),
        name="inproj",
    )(x2, g, w, wg, bg)


def _mlstm_kernel(q_ref, k_ref, v_ref, o_ref, gcol_ref, grow_ref, gain_ref, y_ref, c_ref):
    L = ML_CHUNK

    @pl.when(pl.program_id(1) == 0)
    def _():
        c_ref[...] = jnp.zeros_like(c_ref)

    lane_row = lax.broadcasted_iota(jnp.int32, (1, LANES), 1)
    r_i = lax.broadcasted_iota(jnp.int32, (L, L), 0)
    c_i = lax.broadcasted_iota(jnp.int32, (L, L), 1)
    causal = r_i >= c_i
    gc = gcol_ref[...]
    egc = jnp.exp(gc)
    gain = gain_ref[...]
    heads = range(ML_HEADS)
    pair = lambda h: slice((h // 2) * LANES, (h // 2 + 1) * LANES)
    cols = lambda h: slice(h * ML_DV, (h + 1) * ML_DV)
    col = lambda a, c: a[:, c:c + 1]

    ones = jnp.ones((L, ML_DV), jnp.bfloat16)
    vs = [jnp.concatenate([v_ref[:, cols(h)], ones], axis=1) for h in heads]

    scores, inter = [], []
    for h in heads:
        head_lanes = (lane_row // ML_DQK) == (h % 2)
        q = jnp.where(head_lanes, q_ref[:, pair(h)], jnp.zeros((), jnp.bfloat16))
        scores.append(_dot_nt(q, k_ref[:, pair(h)]))
        inter.append(_dot(q, _bf16(c_ref[h])))

    ps = []
    for h in heads:
        c_row = grow_ref[0, h:h + 1, :] - grow_ref[0, ML_HEADS + h:ML_HEADS + h + 1, :]
        dmat = jnp.where(causal, col(gc, ML_HEADS + h) + c_row, NEG_BIG)
        ps.append(_bf16(scores[h] * jnp.exp(dmat)))

    for h in heads:
        both = _dot(ps[h], vs[h]) + inter[h] * col(egc, ML_HEADS + h)
        hc = both[:, :ML_DV] / jnp.maximum(jnp.abs(both[:, ML_DV:]), 1.0)
        hn = _rms_rows(hc) * gain
        y_ref[:, cols(h)] = _bf16(jax.nn.sigmoid(o_ref[:, cols(h)].astype(jnp.float32)) * hn)

    for h in heads:
        decay = jnp.exp(grow_ref[0, ML_HEADS + h:ML_HEADS + h + 1, L - 1:L])
        kw = k_ref[:, pair(h)].astype(jnp.float32) * col(egc, 2 * ML_HEADS + h)
        c_ref[h] = decay * c_ref[h] + _dot_tn(_bf16(kw), vs[h])


def _mlstm(q, k, v, o, gcol, grow, gain, batch, seq):
    L = ML_CHUNK
    nc = seq // L
    t = batch * seq
    row = lambda n: pl.BlockSpec((L, n), lambda b, c: (b * nc + c, 0))
    return pl.pallas_call(
        _mlstm_kernel,
        grid=(batch, nc),
        in_specs=[row(ML_QK_COLS), row(ML_QK_COLS), row(ML_V_COLS), row(ML_V_COLS), row(GATE_LANES),
                  pl.BlockSpec((1, 3 * ML_HEADS, L), lambda b, c: (b, 0, c)),
                  pl.BlockSpec((1, ML_DV), lambda b, c: (0, 0))],
        out_specs=row(ML_V_COLS),
        out_shape=jax.ShapeDtypeStruct((t, ML_V_COLS), jnp.bfloat16),
        scratch_shapes=[pltpu.VMEM((ML_HEADS, LANES, 2 * ML_DV), jnp.float32)],
        compiler_params=_params(("parallel", "arbitrary")),
        name="mlstm",
    )(q, k, v, o, gcol, grow, gain)


def _post_kernel(h_ref, y_ref, p_ref, wo_ref, gf_ref, wgu_ref, wd_ref, gp_ref, wpg_ref, wpu_ref, out_ref):
    h1 = h_ref[...] + _dot(y_ref[...], wo_ref[...])
    hn = _bf16(_rms_rows(h1) * gf_ref[...])
    gu = _dot(hn, wgu_ref[...])
    act = _bf16(jax.nn.silu(gu[:, :FFN_HIDDEN]) * gu[:, FFN_HIDDEN:])
    h2 = h1 + _dot(act, wd_ref[...])
    hp = _bf16(_rms_rows(h2) * gp_ref[...])
    gate = jax.nn.sigmoid(_dot(hp, wpg_ref[...]))
    up = _dot(_bf16(p_ref[...]), wpu_ref[...])
    out_ref[...] = h2 + up * gate


def _post(h, y, p, layer, wo, gf, wgu, wd, gp, wpg, wpu):
    t = h.shape[0]
    tm = POST_ROW_TILE
    row = lambda n: pl.BlockSpec((tm, n), lambda i: (i, 0))
    p_spec = pl.BlockSpec((None, tm, PLE_DIM), lambda i: (layer, i, 0))
    return pl.pallas_call(
        _post_kernel,
        grid=(t // tm,),
        in_specs=[row(D_MODEL), row(D_MODEL), p_spec, _const_spec(wo.shape), _const_spec((1, D_MODEL)),
                  _const_spec(wgu.shape), _const_spec(wd.shape), _const_spec((1, D_MODEL)),
                  _const_spec(wpg.shape), _const_spec(wpu.shape)],
        out_specs=row(D_MODEL),
        out_shape=jax.ShapeDtypeStruct((t, D_MODEL), jnp.float32),
        compiler_params=_params(("parallel",)),
        name="post",
    )(h, y, p, wo, gf, wgu, wd, gp, wpg, wpu)


def _head_mean_square(x, ones_bd):
    sq = x * x
    hi = _bf16(sq)
    lo = _bf16(sq - hi.astype(jnp.float32))
    return (_dot(hi, ones_bd) + _dot(lo, ones_bd)) * (1.0 / MB_HD)


def _norm_rope(x, ms, gain, cos_t, sin_lo, sin_hi):
    xn = x * lax.rsqrt(ms + EPS) * gain
    outs = []
    for s in range(x.shape[1] // LANES):
        xs = xn[:, s * LANES:(s + 1) * LANES]
        outs.append(xs * cos_t + pltpu.roll(xs, LANES - ROPE_HALF, 1) * sin_lo + pltpu.roll(xs, ROPE_HALF, 1) * sin_hi)
    return jnp.concatenate(outs, axis=1)


def _proj_kernel(h_ref, gq_ref, gkv_ref, wq_ref, wk_ref, wv_ref, qn_ref, kn_ref, bd_ref, cos_ref, slo_ref, shi_ref,
                 q_ref, k_ref, vt_ref, km_ref):
    r = _rms_rows(h_ref[...])
    hq = _bf16(r * gq_ref[...])
    hkv = _bf16(r * gkv_ref[...])
    tables = (cos_ref[...], slo_ref[...], shi_ref[...])
    bd = bd_ref[...]
    kvw = MB_KV_HEADS * MB_HD
    slab = lambda s: slice(s * kvw, (s + 1) * kvw)
    nq = D_MODEL // kvw
    raw = [_dot(hq, wq_ref[:, slab(s)]) for s in range(nq)] + [_dot(hkv, wk_ref[:, slab(s)]) for s in range(2)]
    gains = [qn_ref[:, slab(s)] for s in range(nq)] + [kn_ref[:, slab(s)] for s in range(2)]
    vs = _dot(hkv, wv_ref[...])
    means = [_head_mean_square(x, bd) for x in raw]
    for b in range(vs.shape[0] // MB_BLOCK):
        vt_ref[b] = _bf16(vs[b * MB_BLOCK:(b + 1) * MB_BLOCK].T)
    for s in range(nq):
        q_ref[:, slab(s)] = _bf16(_norm_rope(raw[s], means[s], gains[s], *tables))
    for s in range(2):
        ks = _norm_rope(raw[nq + s], means[nq + s], gains[nq + s], *tables)
        k_ref[:, slab(s)] = _bf16(ks)
        for b in range(ks.shape[0] // MB_BLOCK):
            km_ref[b, :, slab(s)] = jnp.mean(ks[b * MB_BLOCK:(b + 1) * MB_BLOCK], axis=0, keepdims=True)


def _proj(h, gq, gkv, wq, wk2, wv, qn, kn2, bd, cos_t, sin_lo, sin_hi, seq):
    t = h.shape[0]
    tm = ROW_TILE
    kvw = MB_KV_HEADS * MB_HD
    row = lambda n: pl.BlockSpec((tm, n), lambda i: (i, 0))
    tab = pl.BlockSpec((tm, LANES), lambda i: (i % (seq // tm), 0))
    return pl.pallas_call(
        _proj_kernel,
        grid=(t // tm,),
        in_specs=[row(D_MODEL), _const_spec((1, D_MODEL)), _const_spec((1, D_MODEL)), _const_spec(wq.shape),
                  _const_spec(wk2.shape), _const_spec(wv.shape), _const_spec((1, D_MODEL)), _const_spec((1, 2 * kvw)),
                  _const_spec(bd.shape), tab, tab, tab],
        out_specs=[row(D_MODEL), row(2 * kvw),
                   pl.BlockSpec((tm // MB_BLOCK, kvw, MB_BLOCK), lambda i: (i, 0, 0)),
                   pl.BlockSpec((tm // MB_BLOCK, 1, 2 * kvw), lambda i: (i, 0, 0))],
        out_shape=[jax.ShapeDtypeStruct((t, D_MODEL), jnp.bfloat16),
                   jax.ShapeDtypeStruct((t, 2 * kvw), jnp.bfloat16),
                   jax.ShapeDtypeStruct((t // MB_BLOCK, kvw, MB_BLOCK), jnp.bfloat16),
                   jax.ShapeDtypeStruct((t // MB_BLOCK, 1, 2 * kvw), jnp.float32)],
        compiler_params=_params(("parallel",)),
        name="proj",
    )(h, gq, gkv, wq, wk2, wv, qn, kn2, bd, cos_t, sin_lo, sin_hi)


def _moba_kernel(q_ref, k_ref, vt_ref, km_ref, o_ref, qa_ref, sel_ref, sa_ref, sb_ref, mxa_ref, mxb_ref, acc_ref):
    i = pl.program_id(2)
    nb = km_ref.shape[0]
    heads = range(MB_GROUP)
    first_half = lax.broadcasted_iota(jnp.int32, (1, LANES), 1) < MB_HD
    zero = jnp.zeros((), jnp.bfloat16)
    for u in range(MB_GROUP // 2):
        t = q_ref[:, u * LANES:(u + 1) * LANES]
        qa_ref[2 * u] = jnp.where(first_half, t, zero)
        qa_ref[2 * u + 1] = jnp.where(first_half, zero, t)
    km = km_ref[:, 0, :]
    km_hi = _bf16(km)
    km_lo = _bf16(km - km_hi.astype(jnp.float32))
    blk = lax.broadcasted_iota(jnp.int32, (nb, MB_BLOCK), 0)
    past = blk < i
    key_i = lax.broadcasted_iota(jnp.int32, (MB_BLOCK, MB_BLOCK), 0)
    qry_i = lax.broadcasted_iota(jnp.int32, (MB_BLOCK, MB_BLOCK), 1)
    causal = key_i <= qry_i

    def scores_into(s_ref, mx_ref, n):
        kn = k_ref[n]
        for h in heads:
            s = _dot_nt(kn, qa_ref[h])
            s_ref[h] = s
            mx_ref[h] = jnp.max(s, axis=0, keepdims=True)

    own_and_means = jnp.concatenate([k_ref[i], km_hi, km_lo], axis=0)
    own, gates = [], []
    for h in heads:
        r = _dot_nt(own_and_means, qa_ref[h])
        own.append(jnp.where(causal, r[:MB_BLOCK], NEG_BIG))
        gates.append(jnp.where(past, r[MB_BLOCK:MB_BLOCK + nb] + r[MB_BLOCK + nb:], -jnp.inf))
    scores_into(sa_ref, mxa_ref, 0)
    for h in heads:
        g = gates[h]
        chosen = jnp.zeros((nb, MB_BLOCK), jnp.bool_)
        for _ in range(MB_TOPK):
            top = jnp.max(g, axis=0, keepdims=True)
            first = jnp.min(jnp.where(g == top, blk, nb), axis=0, keepdims=True)
            pick = blk == first
            chosen = chosen | pick
            g = jnp.where(pick, -jnp.inf, g)
        sel_ref[h] = (chosen & past).astype(jnp.float32)

    ones_rows = jnp.ones((16, MB_BLOCK), jnp.bfloat16)
    values = lambda n: jnp.concatenate([vt_ref[n], ones_rows], axis=0)

    def weights(s, offset):
        return jnp.exp2(_bf16(s - offset))

    vt_own = values(i)
    ms, ls = [], []
    for h in heads:
        mx = jnp.max(own[h], axis=0, keepdims=True)
        pv = _dot(vt_own, weights(own[h], mx))
        ms.append(mx)
        ls.append(pv[MB_HD:MB_HD + 1, :])
        acc_ref[h * MB_HD:(h + 1) * MB_HD, :] = pv[:MB_HD, :]

    def attend(s_ref, mx_ref, n, carry):
        ms, ls = carry
        vtn = values(n)
        new_ms, new_ls = [], []
        for h in heads:
            chosen = sel_ref[h, pl.ds(n, 1), :] > 0.5
            m_new = jnp.maximum(ms[h], jnp.where(chosen, mx_ref[h], NEG_BIG))
            alpha = jnp.exp2(ms[h] - m_new)
            pv = _dot(vtn, weights(s_ref[h], jnp.where(chosen, m_new, -NEG_BIG)))
            new_ms.append(m_new)
            new_ls.append(alpha * ls[h] + pv[MB_HD:MB_HD + 1, :])
            rows = slice(h * MB_HD, (h + 1) * MB_HD)
            acc_ref[rows, :] = alpha * acc_ref[rows, :] + pv[:MB_HD, :]
        return tuple(new_ms), tuple(new_ls)

    def pair(n, carry):
        scores_into(sb_ref, mxb_ref, n + 1)
        carry = attend(sa_ref, mxa_ref, n, carry)
        scores_into(sa_ref, mxa_ref, jnp.minimum(n + 2, nb - 1))
        return attend(sb_ref, mxb_ref, n + 1, carry)

    def quad(j, carry):
        return pair(4 * j + 2, pair(4 * j, carry))

    keep = lambda c: c
    carry = lax.fori_loop(0, i // 4, quad, (tuple(ms), tuple(ls)))
    carry = lax.cond(i % 4 >= 2, lambda c: pair(4 * (i // 4), c), keep, carry)
    ms, ls = lax.cond(i % 2 == 1, lambda c: attend(sa_ref, mxa_ref, i - 1, c), keep, carry)

    for h in heads:
        rows = slice(h * MB_HD, (h + 1) * MB_HD)
        acc_ref[rows, :] = acc_ref[rows, :] / ls[h]
    o_ref[...] = _bf16(acc_ref[...].T)


def _moba(q, k, vt, km, batch, seq):
    nb = seq // MB_BLOCK
    t = batch * seq
    gw = MB_GROUP * MB_HD
    return pl.pallas_call(
        _moba_kernel,
        grid=(batch, MB_KV_HEADS, nb),
        in_specs=[pl.BlockSpec((MB_BLOCK, gw), lambda b, g, i: (b * nb + i, g)),
                  pl.BlockSpec((nb, MB_BLOCK, LANES), lambda b, g, i: (b, 0, g)),
                  pl.BlockSpec((nb, MB_HD, MB_BLOCK), lambda b, g, i: (b, g, 0)),
                  pl.BlockSpec((nb, 1, LANES), lambda b, g, i: (b, 0, g))],
        out_specs=pl.BlockSpec((MB_BLOCK, gw), lambda b, g, i: (b * nb + i, g)),
        out_shape=jax.ShapeDtypeStruct((t, MB_HEADS * MB_HD), jnp.bfloat16),
        scratch_shapes=[pltpu.VMEM((MB_GROUP, MB_BLOCK, LANES), jnp.bfloat16),
                        pltpu.VMEM((MB_GROUP, nb, MB_BLOCK), jnp.float32),
                        pltpu.VMEM((MB_GROUP, MB_BLOCK, MB_BLOCK), jnp.float32),
                        pltpu.VMEM((MB_GROUP, MB_BLOCK, MB_BLOCK), jnp.float32),
                        pltpu.VMEM((MB_GROUP, 1, MB_BLOCK), jnp.float32),
                        pltpu.VMEM((MB_GROUP, 1, MB_BLOCK), jnp.float32),
                        pltpu.VMEM((gw, MB_BLOCK), jnp.float32)],
        compiler_params=_params(("parallel", "parallel", "arbitrary")),
        name="moba",
    )(q, k, vt, km)


def _rope_tables(seq):
    inv = ROPE_THETA ** (-jnp.arange(0, ROPE_DIM, 2, dtype=jnp.float32) / ROPE_DIM)
    ang = jnp.arange(seq, dtype=jnp.float32)[:, None] * inv[None, :]
    cos, sin = jnp.cos(ang), jnp.sin(ang)
    zeros = jnp.zeros((seq, MB_HD - ROPE_DIM), jnp.float32)
    zh = jnp.zeros((seq, ROPE_HALF), jnp.float32)
    cos_h = jnp.concatenate([cos, cos, jnp.ones_like(zeros)], axis=1)
    lo_h = jnp.concatenate([-sin, zh, zeros], axis=1)
    hi_h = jnp.concatenate([zh, sin, zeros], axis=1)
    two = lambda a: jnp.concatenate([a, a], axis=1)
    return two(cos_h), two(lo_h), two(hi_h)


def kernel(x, p, norm_mix, a_w_in, a_b_gate, a_mh_gain, a_w_out, kv_norm, w_kv, k_norm, b_w_q, b_q_norm, b_w_o,
           norm_ffn, w_gate_up, w_down, norm_ple, w_ple_gate, w_ple_up):
    batch, seq, d = x.shape
    t = batch * seq
    f32 = jnp.float32
    x2 = x.reshape(t, d)
    p2 = p.reshape(p.shape[0], t, PLE_DIM)
    row = lambda a: a.reshape(1, -1).astype(f32)

    w_in = a_w_in[0]
    n_main = 2 * ML_QK_COLS + 2 * ML_V_COLS
    w_main = _bf16(w_in[:, :n_main])
    w_gate = _bf16(jnp.pad(w_in[:, n_main:], ((0, 0), (0, GATE_LANES - 2 * ML_HEADS))))
    b_gate = jnp.pad(a_b_gate[0].astype(f32), (0, GATE_LANES - 2 * ML_HEADS)).reshape(1, GATE_LANES)
    q, k, v, o, gcol = _inproj(x2, row(norm_mix[0]), w_main, w_gate, b_gate)
    grow = jnp.transpose(gcol.reshape(batch, seq, GATE_LANES)[:, :, :3 * ML_HEADS], (0, 2, 1))
    y = _mlstm(q, k, v, o, gcol, grow, row(a_mh_gain[0]), batch, seq)
    h = _post(x2, y, p2, 0, _bf16(a_w_out[0]), row(norm_ffn[0]), _bf16(w_gate_up[0]), _bf16(w_down[0]),
              row(norm_ple[0]), _bf16(w_ple_gate[0]), _bf16(w_ple_up[0]))

    nb = seq // MB_BLOCK
    kvw = MB_KV_HEADS * MB_HD
    cos_t, sin_lo, sin_hi = _rope_tables(seq)
    head = lax.broadcasted_iota(jnp.int32, (kvw, kvw), 0) // MB_HD
    ones_bd = (head == head.T).astype(jnp.bfloat16)
    qn = row(jnp.tile(b_q_norm[0], MB_HEADS)) * MB_Q_SCALE
    kn2 = row(jnp.tile(k_norm, 2 * MB_KV_HEADS))
    w_k = _bf16(w_kv[:, :kvw]).reshape(d, MB_KV_HEADS, 1, MB_HD)
    w_k2 = jnp.broadcast_to(w_k, (d, MB_KV_HEADS, 2, MB_HD)).reshape(d, 2 * kvw)
    q, k2, vt, km2 = _proj(h, row(norm_mix[1]), row(kv_norm), _bf16(b_w_q[0]), w_k2, _bf16(w_kv[:, kvw:]), qn, kn2,
                           ones_bd, cos_t, sin_lo, sin_hi, seq)
    y = _moba(q, k2.reshape(batch * nb, MB_BLOCK, 2 * kvw), vt, km2, batch, seq)
    h = _post(h, y, p2, 1, _bf16(b_w_o[0]), row(norm_ffn[1]), _bf16(w_gate_up[1]), _bf16(w_down[1]),
              row(norm_ple[1]), _bf16(w_ple_gate[1]), _bf16(w_ple_up[1]))
    return h.reshape(batch, seq, d)
```

```python
import functools
import math

import jax
import jax.numpy as jnp
from jax import lax
from jax.experimental import pallas as pl
from jax.experimental.pallas import tpu as pltpu

D_MODEL = 1024
PLE_DIM = 256
EPS = 1e-6

ML_HEADS = 8
ML_DV = 128
ML_DQK = 64
GATE_CAP = 15.0
ML_QK_COLS = ML_HEADS * ML_DQK
ML_V_COLS = ML_HEADS * ML_DV
ML_CHUNK = 256

MB_HEADS = 16
MB_KV_HEADS = 4
MB_HD = 64
MB_GROUP = MB_HEADS // MB_KV_HEADS
MB_BLOCK = 256
MB_TOPK = 3
ROPE_THETA = 500000.0
ROPE_DIM = MB_HD // 4
ROPE_HALF = ROPE_DIM // 2
MB_Q_SCALE = MB_HD ** -0.5 * math.log2(math.e)

FFN_HIDDEN = 2816

LANES = 128
GATE_LANES = LANES
NEG_BIG = -1e30
VMEM_LIMIT = 56 * 1024 * 1024

ROW_TILE = 512
POST_ROW_TILE = 256


def _bf16(x):
    return x.astype(jnp.bfloat16)


def _dot(a, b):
    return jnp.dot(a, b, preferred_element_type=jnp.float32)


def _dot_nt(a, b):
    return lax.dot_general(a, b, (((1,), (1,)), ((), ())), preferred_element_type=jnp.float32)


def _dot_tn(a, b):
    return lax.dot_general(a, b, (((0,), (0,)), ((), ())), preferred_element_type=jnp.float32)


def _rms_rows(x):
    return x * lax.rsqrt(jnp.mean(x * x, axis=-1, keepdims=True) + EPS)


def _const_spec(shape):
    nd = len(shape)
    return pl.BlockSpec(shape, lambda *_: (0,) * nd, pipeline_mode=pl.Buffered(1))


def _params(sem):
    return pltpu.CompilerParams(dimension_semantics=sem, vmem_limit_bytes=VMEM_LIMIT)


def _inproj_kernel(x_ref, g_ref, w_ref, wg_ref, bg_ref, q_ref, k_ref, v_ref, o_ref, gate_ref):
    hn = _bf16(_rms_rows(x_ref[...]) * g_ref[...])
    c0, c1, c2, c3 = ML_QK_COLS, 2 * ML_QK_COLS, 2 * ML_QK_COLS + ML_V_COLS, 2 * ML_QK_COLS + 2 * ML_V_COLS
    q_ref[...] = _bf16(_dot(hn, w_ref[:, 0:c0]) * (ML_DQK ** -0.5))
    k_ref[...] = _bf16(_dot(hn, w_ref[:, c0:c1]))
    v_ref[...] = _bf16(_dot(hn, w_ref[:, c1:c2]))
    o_ref[...] = _bf16(_dot(hn, w_ref[:, c2:c3]))

    pre = _dot(hn, wg_ref[...]) + bg_ref[...]
    cap = GATE_CAP * jnp.tanh(pre * (1.0 / GATE_CAP))
    logf = jnp.minimum(cap, 0.0) - jnp.log1p(jnp.exp(-jnp.abs(cap)))
    rows = cap.shape[0]
    row_in_chunk = lax.broadcasted_iota(jnp.int32, cap.shape, 0) % ML_CHUNK
    cum = logf
    d = 1
    while d < ML_CHUNK:
        cum = cum + jnp.where(row_in_chunk >= d, pltpu.roll(cum, d, 0), 0.0)
        d *= 2
    li_shift = pltpu.roll(cap, ML_HEADS, 1)
    parts = []
    for c in range(rows // ML_CHUNK):
        sl = slice(c * ML_CHUNK, (c + 1) * ML_CHUNK)
        g_last = cum[(c + 1) * ML_CHUNK - 1:(c + 1) * ML_CHUNK, :]
        parts.append(g_last - cum[sl] + li_shift[sl])
    wlog = pltpu.roll(jnp.concatenate(parts, axis=0), ML_HEADS, 1)
    lane = lax.broadcasted_iota(jnp.int32, cap.shape, 1)
    gate_ref[...] = jnp.where(lane < ML_HEADS, cap, jnp.where(lane < 2 * ML_HEADS, cum, wlog))


def _inproj(x2, g, w, wg, bg):
    t = x2.shape[0]
    tm = ROW_TILE
    row = lambda n: pl.BlockSpec((tm, n), lambda i: (i, 0))
    return pl.pallas_call(
        _inproj_kernel,
        grid=(t // tm,),
        in_specs=[row(D_MODEL), _const_spec((1, D_MODEL)), _const_spec(w.shape), _const_spec(wg.shape),
                  _const_spec((1, GATE_LANES))],
        out_specs=[row(ML_QK_COLS), row(ML_QK_COLS), row(ML_V_COLS), row(ML_V_COLS), row(GATE_LANES)],
        out_shape=[jax.ShapeDtypeStruct((t, ML_QK_COLS), jnp.bfloat16),
                   jax.ShapeDtypeStruct((t, ML_QK_COLS), jnp.bfloat16),
                   jax.ShapeDtypeStruct((t, ML_V_COLS), jnp.bfloat16),
                   jax.ShapeDtypeStruct((t, ML_V_COLS), jnp.bfloat16),
                   jax.ShapeDtypeStruct((t, GATE_LANES), jnp.float32)],
        compiler_params=_params(("parallel",)),
        name="inproj",
    )(x2, g, w, wg, bg)


def _mlstm_kernel(q_ref, k_ref, v_ref, o_ref, gcol_ref, grow_ref, gain_ref, y_ref, c_ref):
    L = ML_CHUNK

    @pl.when(pl.program_id(1) == 0)
    def _():
        c_ref[...] = jnp.zeros_like(c_ref)

    lane_row = lax.broadcasted_iota(jnp.int32, (1, LANES), 1)
    r_i = lax.broadcasted_iota(jnp.int32, (L, L), 0)
    c_i = lax.broadcasted_iota(jnp.int32, (L, L), 1)
    causal = r_i >= c_i
    gc = gcol_ref[...]
    egc = jnp.exp(gc)
    gain = gain_ref[...]
    heads = range(ML_HEADS)
    pair = lambda h: slice((h // 2) * LANES, (h // 2 + 1) * LANES)
    cols = lambda h: slice(h * ML_DV, (h + 1) * ML_DV)
    col = lambda a, c: a[:, c:c + 1]

    ones = jnp.ones((L, ML_DV), jnp.bfloat16)
    vs = [jnp.concatenate([v_ref[:, cols(h)], ones], axis=1) for h in heads]

    scores, inter = [], []
    for h in heads:
        head_lanes = (lane_row // ML_DQK) == (h % 2)
        q = jnp.where(head_lanes, q_ref[:, pair(h)], jnp.zeros((), jnp.bfloat16))
        scores.append(_dot_nt(q, k_ref[:, pair(h)]))
        inter.append(_dot(q, _bf16(c_ref[h])))

    ps = []
    for h in heads:
        c_row = grow_ref[0, h:h + 1, :] - grow_ref[0, ML_HEADS + h:ML_HEADS + h + 1, :]
        dmat = jnp.where(causal, col(gc, ML_HEADS + h) + c_row, NEG_BIG)
        ps.append(_bf16(scores[h] * jnp.exp(dmat)))

    for h in heads:
        both = _dot(ps[h], vs[h]) + inter[h] * col(egc, ML_HEADS + h)
        hc = both[:, :ML_DV] / jnp.maximum(jnp.abs(both[:, ML_DV:]), 1.0)
        hn = _rms_rows(hc) * gain
        y_ref[:, cols(h)] = _bf16(jax.nn.sigmoid(o_ref[:, cols(h)].astype(jnp.float32)) * hn)

    for h in heads:
        decay = jnp.exp(grow_ref[0, ML_HEADS + h:ML_HEADS + h + 1, L - 1:L])
        kw = k_ref[:, pair(h)].astype(jnp.float32) * col(egc, 2 * ML_HEADS + h)
        c_ref[h] = decay * c_ref[h] + _dot_tn(_bf16(kw), vs[h])


def _mlstm(q, k, v, o, gcol, grow, gain, batch, seq):
    L = ML_CHUNK
    nc = seq // L
    t = batch * seq
    row = lambda n: pl.BlockSpec((L, n), lambda b, c: (b * nc + c, 0))
    return pl.pallas_call(
        _mlstm_kernel,
        grid=(batch, nc),
        in_specs=[row(ML_QK_COLS), row(ML_QK_COLS), row(ML_V_COLS), row(ML_V_COLS), row(GATE_LANES),
                  pl.BlockSpec((1, 3 * ML_HEADS, L), lambda b, c: (b, 0, c)),
                  pl.BlockSpec((1, ML_DV), lambda b, c: (0, 0))],
        out_specs=row(ML_V_COLS),
        out_shape=jax.ShapeDtypeStruct((t, ML_V_COLS), jnp.bfloat16),
        scratch_shapes=[pltpu.VMEM((ML_HEADS, LANES, 2 * ML_DV), jnp.float32)],
        compiler_params=_params(("parallel", "arbitrary")),
        name="mlstm",
    )(q, k, v, o, gcol, grow, gain)


def _post_kernel(h_ref, y_ref, p_ref, wo_ref, gf_ref, wgu_ref, wd_ref, gp_ref, wpg_ref, wpu_ref, out_ref):
    h1 = h_ref[...] + _dot(y_ref[...], wo_ref[...])
    hn = _bf16(_rms_rows(h1) * gf_ref[...])
    gu = _dot(hn, wgu_ref[...])
    act = _bf16(jax.nn.silu(gu[:, :FFN_HIDDEN]) * gu[:, FFN_HIDDEN:])
    h2 = h1 + _dot(act, wd_ref[...])
    hp = _bf16(_rms_rows(h2) * gp_ref[...])
    gate = jax.nn.sigmoid(_dot(hp, wpg_ref[...]))
    up = _dot(_bf16(p_ref[...]), wpu_ref[...])
    out_ref[...] = h2 + up * gate


def _post(h, y, p, layer, wo, gf, wgu, wd, gp, wpg, wpu):
    t = h.shape[0]
    tm = POST_ROW_TILE
    row = lambda n: pl.BlockSpec((tm, n), lambda i: (i, 0))
    p_spec = pl.BlockSpec((None, tm, PLE_DIM), lambda i: (layer, i, 0))
    return pl.pallas_call(
        _post_kernel,
        grid=(t // tm,),
        in_specs=[row(D_MODEL), row(D_MODEL), p_spec, _const_spec(wo.shape), _const_spec((1, D_MODEL)),
                  _const_spec(wgu.shape), _const_spec(wd.shape), _const_spec((1, D_MODEL)),
                  _const_spec(wpg.shape), _const_spec(wpu.shape)],
        out_specs=row(D_MODEL),
        out_shape=jax.ShapeDtypeStruct((t, D_MODEL), jnp.float32),
        compiler_params=_params(("parallel",)),
        name="post",
    )(h, y, p, wo, gf, wgu, wd, gp, wpg, wpu)


def _head_mean_square(x, ones_bd):
    sq = x * x
    hi = _bf16(sq)
    lo = _bf16(sq - hi.astype(jnp.float32))
    return (_dot(hi, ones_bd) + _dot(lo, ones_bd)) * (1.0 / MB_HD)


def _norm_rope(x, ms, gain, cos_t, sin_lo, sin_hi):
    xn = x * lax.rsqrt(ms + EPS) * gain
    outs = []
    for s in range(x.shape[1] // LANES):
        xs = xn[:, s * LANES:(s + 1) * LANES]
        outs.append(xs * cos_t + pltpu.roll(xs, LANES - ROPE_HALF, 1) * sin_lo + pltpu.roll(xs, ROPE_HALF, 1) * sin_hi)
    return jnp.concatenate(outs, axis=1)


def _proj_kernel(h_ref, gq_ref, gkv_ref, wq_ref, wk_ref, wv_ref, qn_ref, kn_ref, bd_ref, cos_ref, slo_ref, shi_ref,
                 q_ref, k_ref, vt_ref, km_ref):
    r = _rms_rows(h_ref[...])
    hq = _bf16(r * gq_ref[...])
    hkv = _bf16(r * gkv_ref[...])
    tables = (cos_ref[...], slo_ref[...], shi_ref[...])
    bd = bd_ref[...]
    kvw = MB_KV_HEADS * MB_HD
    slab = lambda s: slice(s * kvw, (s + 1) * kvw)
    nq = D_MODEL // kvw
    raw = [_dot(hq, wq_ref[:, slab(s)]) for s in range(nq)] + [_dot(hkv, wk_ref[:, slab(s)]) for s in range(2)]
    gains = [qn_ref[:, slab(s)] for s in range(nq)] + [kn_ref[:, slab(s)] for s in range(2)]
    vs = _dot(hkv, wv_ref[...])
    means = [_head_mean_square(x, bd) for x in raw]
    vt_ref[0] = _bf16(vs.T)
    for s in range(nq):
        q_ref[:, slab(s)] = _bf16(_norm_rope(raw[s], means[s], gains[s], *tables))
    for s in range(2):
        ks = _norm_rope(raw[nq + s], means[nq + s], gains[nq + s], *tables)
        k_ref[:, slab(s)] = _bf16(ks)
        for b in range(ks.shape[0] // MB_BLOCK):
            km_ref[b, :, slab(s)] = jnp.mean(ks[b * MB_BLOCK:(b + 1) * MB_BLOCK], axis=0, keepdims=True)


def _proj(h, gq, gkv, wq, wk2, wv, qn, kn2, bd, cos_t, sin_lo, sin_hi, seq):
    t = h.shape[0]
    tm = ROW_TILE
    kvw = MB_KV_HEADS * MB_HD
    row = lambda n: pl.BlockSpec((tm, n), lambda i: (i, 0))
    tab = pl.BlockSpec((tm, LANES), lambda i: (i % (seq // tm), 0))
    return pl.pallas_call(
        _proj_kernel,
        grid=(t // tm,),
        in_specs=[row(D_MODEL), _const_spec((1, D_MODEL)), _const_spec((1, D_MODEL)), _const_spec(wq.shape),
                  _const_spec(wk2.shape), _const_spec(wv.shape), _const_spec((1, D_MODEL)), _const_spec((1, 2 * kvw)),
                  _const_spec(bd.shape), tab, tab, tab],
        out_specs=[row(D_MODEL), row(2 * kvw),
                   pl.BlockSpec((1, kvw, tm), lambda i: (i // (seq // tm), 0, i % (seq // tm))),
                   pl.BlockSpec((tm // MB_BLOCK, 1, 2 * kvw), lambda i: (i, 0, 0))],
        out_shape=[jax.ShapeDtypeStruct((t, D_MODEL), jnp.bfloat16),
                   jax.ShapeDtypeStruct((t, 2 * kvw), jnp.bfloat16),
                   jax.ShapeDtypeStruct((t // seq, kvw, seq), jnp.bfloat16),
                   jax.ShapeDtypeStruct((t // MB_BLOCK, 1, 2 * kvw), jnp.float32)],
        compiler_params=_params(("parallel",)),
        name="proj",
    )(h, gq, gkv, wq, wk2, wv, qn, kn2, bd, cos_t, sin_lo, sin_hi)


def _moba_kernel(q_ref, k_ref, vt_ref, km_ref, _, o_ref, qa_ref, s_ref, acc_ref, vaug_ref, *, i):
    nb = km_ref.shape[1]
    heads = range(MB_GROUP)
    nkeys = (i + 1) * MB_BLOCK
    first_half = lax.broadcasted_iota(jnp.int32, (1, LANES), 1) < MB_HD
    zero = jnp.zeros((), jnp.bfloat16)
    for u in range(MB_GROUP // 2):
        t = q_ref[:, u * LANES:(u + 1) * LANES]
        qa_ref[2 * u] = jnp.where(first_half, t, zero)
        qa_ref[2 * u + 1] = jnp.where(first_half, zero, t)
    vaug_ref[:MB_HD, :] = vt_ref[0]
    vaug_ref[MB_HD:, :] = jnp.ones((vaug_ref.shape[0] - MB_HD, nkeys), jnp.bfloat16)

    km = km_ref[0, :, 0, :]
    km_hi = _bf16(km)
    km_lo = _bf16(km - km_hi.astype(jnp.float32))
    blk = lax.broadcasted_iota(jnp.int32, (nb, MB_BLOCK), 0)
    past = blk < i
    key_i = lax.broadcasted_iota(jnp.int32, (MB_BLOCK, MB_BLOCK), 0)
    qry_i = lax.broadcasted_iota(jnp.int32, (MB_BLOCK, MB_BLOCK), 1)
    causal = key_i <= qry_i
    own_and_means = jnp.concatenate([k_ref[0, i], km_hi, km_lo], axis=0)

    def scores_of(n):
        maxes = []
        for h in heads:
            s = _dot_nt(k_ref[0, n], qa_ref[h])
            s_ref[h, n] = s
            maxes.append(jnp.max(s, axis=0, keepdims=True))
        return maxes

    def weights(s, offset):
        return jnp.exp2(_bf16(s - offset))

    own, chosen = [], []
    for h in heads:
        r = _dot_nt(own_and_means, qa_ref[h])
        own.append(jnp.where(causal, r[:MB_BLOCK], NEG_BIG))
        g = jnp.where(past, r[MB_BLOCK:MB_BLOCK + nb] + r[MB_BLOCK + nb:], -jnp.inf)
        picked = jnp.zeros((nb, MB_BLOCK), jnp.bool_)
        for _ in range(MB_TOPK):
            top = jnp.max(g, axis=0, keepdims=True)
            first = jnp.min(jnp.where(g == top, blk, nb), axis=0, keepdims=True)
            pick = blk == first
            picked = picked | pick
            g = jnp.where(pick, -jnp.inf, g)
        chosen.append(picked & past)
    block_max = scores_of(0) if i > 0 else None

    ms, ls = [], []
    for h in heads:
        mx = jnp.max(own[h], axis=0, keepdims=True)
        pv = _dot(vaug_ref[:, i * MB_BLOCK:nkeys], weights(own[h], mx))
        ms.append(mx)
        ls.append(pv[MB_HD:MB_HD + 1, :])
        acc_ref[h * MB_HD:(h + 1) * MB_HD, :] = pv[:MB_HD, :]

    for n in range(i):
        next_max = scores_of(n + 1) if n + 1 < i else None
        vtn = vaug_ref[:, n * MB_BLOCK:(n + 1) * MB_BLOCK]
        for h in heads:
            use = chosen[h][n:n + 1]
            m_new = jnp.maximum(ms[h], jnp.where(use, block_max[h], NEG_BIG))
            alpha = jnp.exp2(ms[h] - m_new)
            pv = _dot(vtn, weights(s_ref[h, n], jnp.where(use, m_new, -NEG_BIG)))
            ms[h] = m_new
            ls[h] = alpha * ls[h] + pv[MB_HD:MB_HD + 1, :]
            rows = slice(h * MB_HD, (h + 1) * MB_HD)
            acc_ref[rows, :] = alpha * acc_ref[rows, :] + pv[:MB_HD, :]
        block_max = next_max

    for h in heads:
        rows = slice(h * MB_HD, (h + 1) * MB_HD)
        acc_ref[rows, :] = acc_ref[rows, :] / ls[h]
    o_ref[...] = _bf16(acc_ref[...].T)


def _moba(q, k, vt, km, batch, seq):
    nb = seq // MB_BLOCK
    t = batch * seq
    gw = MB_GROUP * MB_HD
    y = jnp.zeros((t, MB_HEADS * MB_HD), jnp.bfloat16)
    for i in range(nb):
        nkeys = (i + 1) * MB_BLOCK
        y = pl.pallas_call(
            functools.partial(_moba_kernel, i=i),
            grid=(batch, MB_KV_HEADS),
            in_specs=[pl.BlockSpec((MB_BLOCK, gw), lambda b, g, i=i: (b * nb + i, g)),
                      pl.BlockSpec((1, i + 1, MB_BLOCK, LANES), lambda b, g: (b, 0, 0, g)),
                      pl.BlockSpec((1, MB_HD, nkeys), lambda b, g: (b, g, 0)),
                      pl.BlockSpec((1, nb, 1, LANES), lambda b, g: (b, 0, 0, g)),
                      pl.BlockSpec(memory_space=pl.ANY)],
            out_specs=pl.BlockSpec((MB_BLOCK, gw), lambda b, g, i=i: (b * nb + i, g)),
            out_shape=jax.ShapeDtypeStruct((t, MB_HEADS * MB_HD), jnp.bfloat16),
            input_output_aliases={4: 0},
            scratch_shapes=[pltpu.VMEM((MB_GROUP, MB_BLOCK, LANES), jnp.bfloat16),
                            pltpu.VMEM((MB_GROUP, max(i, 1), MB_BLOCK, MB_BLOCK), jnp.float32),
                            pltpu.VMEM((gw, MB_BLOCK), jnp.float32),
                            pltpu.VMEM((MB_HD + 16, nkeys), jnp.bfloat16)],
            compiler_params=_params(("parallel", "parallel")),
            name=f"moba{i}",
        )(q, k, vt, km, y)
    return y


def _rope_tables(seq):
    inv = ROPE_THETA ** (-jnp.arange(0, ROPE_DIM, 2, dtype=jnp.float32) / ROPE_DIM)
    ang = jnp.arange(seq, dtype=jnp.float32)[:, None] * inv[None, :]
    cos, sin = jnp.cos(ang), jnp.sin(ang)
    zeros = jnp.zeros((seq, MB_HD - ROPE_DIM), jnp.float32)
    zh = jnp.zeros((seq, ROPE_HALF), jnp.float32)
    cos_h = jnp.concatenate([cos, cos, jnp.ones_like(zeros)], axis=1)
    lo_h = jnp.concatenate([-sin, zh, zeros], axis=1)
    hi_h = jnp.concatenate([zh, sin, zeros], axis=1)
    two = lambda a: jnp.concatenate([a, a], axis=1)
    return two(cos_h), two(lo_h), two(hi_h)


def kernel(x, p, norm_mix, a_w_in, a_b_gate, a_mh_gain, a_w_out, kv_norm, w_kv, k_norm, b_w_q, b_q_norm, b_w_o,
           norm_ffn, w_gate_up, w_down, norm_ple, w_ple_gate, w_ple_up):
    batch, seq, d = x.shape
    t = batch * seq
    f32 = jnp.float32
    x2 = x.reshape(t, d)
    p2 = p.reshape(p.shape[0], t, PLE_DIM)
    row = lambda a: a.reshape(1, -1).astype(f32)

    w_in = a_w_in[0]
    n_main = 2 * ML_QK_COLS + 2 * ML_V_COLS
    w_main = _bf16(w_in[:, :n_main])
    w_gate = _bf16(jnp.pad(w_in[:, n_main:], ((0, 0), (0, GATE_LANES - 2 * ML_HEADS))))
    b_gate = jnp.pad(a_b_gate[0].astype(f32), (0, GATE_LANES - 2 * ML_HEADS)).reshape(1, GATE_LANES)
    q, k, v, o, gcol = _inproj(x2, row(norm_mix[0]), w_main, w_gate, b_gate)
    grow = jnp.transpose(gcol.reshape(batch, seq, GATE_LANES)[:, :, :3 * ML_HEADS], (0, 2, 1))
    y = _mlstm(q, k, v, o, gcol, grow, row(a_mh_gain[0]), batch, seq)
    h = _post(x2, y, p2, 0, _bf16(a_w_out[0]), row(norm_ffn[0]), _bf16(w_gate_up[0]), _bf16(w_down[0]),
              row(norm_ple[0]), _bf16(w_ple_gate[0]), _bf16(w_ple_up[0]))

    nb = seq // MB_BLOCK
    kvw = MB_KV_HEADS * MB_HD
    cos_t, sin_lo, sin_hi = _rope_tables(seq)
    head = lax.broadcasted_iota(jnp.int32, (kvw, kvw), 0) // MB_HD
    ones_bd = (head == head.T).astype(jnp.bfloat16)
    qn = row(jnp.tile(b_q_norm[0], MB_HEADS)) * MB_Q_SCALE
    kn2 = row(jnp.tile(k_norm, 2 * MB_KV_HEADS))
    w_k = _bf16(w_kv[:, :kvw]).reshape(d, MB_KV_HEADS, 1, MB_HD)
    w_k2 = jnp.broadcast_to(w_k, (d, MB_KV_HEADS, 2, MB_HD)).reshape(d, 2 * kvw)
    q, k2, vt, km2 = _proj(h, row(norm_mix[1]), row(kv_norm), _bf16(b_w_q[0]), w_k2, _bf16(w_kv[:, kvw:]), qn, kn2,
                           ones_bd, cos_t, sin_lo, sin_hi, seq)
    y = _moba(q, k2.reshape(batch, nb, MB_BLOCK, 2 * kvw), vt, km2.reshape(batch, nb, 1, 2 * kvw), batch, seq)
    h = _post(h, y, p2, 1, _bf16(b_w_o[0]), row(norm_ffn[1]), _bf16(w_gate_up[1]), _bf16(w_down[1]),
              row(norm_ple[1]), _bf16(w_ple_gate[1]), _bf16(w_ple_up[1]))
    return h.reshape(batch, seq, d)
```

```python
import functools
import math

import jax
import jax.numpy as jnp
from jax import lax
from jax.experimental import pallas as pl
from jax.experimental.pallas import tpu as pltpu
import numpy as np

D_MODEL = 1024
PLE_DIM = 256
EPS = 1e-6

ML_HEADS = 8
ML_DV = 128
ML_DQK = 64
GATE_CAP = 15.0
ML_QK_COLS = ML_HEADS * ML_DQK
ML_V_COLS = ML_HEADS * ML_DV
ML_CHUNK = 256

MB_HEADS = 16
MB_KV_HEADS = 4
MB_HD = 64
MB_GROUP = MB_HEADS // MB_KV_HEADS
MB_BLOCK = 256
MB_TOPK = 3
ROPE_THETA = 500000.0
ROPE_DIM = MB_HD // 4
ROPE_HALF = ROPE_DIM // 2
MB_Q_SCALE = MB_HD ** -0.5 * math.log2(math.e)

FFN_HIDDEN = 2816

LANES = 128
GATE_LANES = LANES
NEG_BIG = -1e30
VMEM_LIMIT = 56 * 1024 * 1024

ROW_TILE = 512
POST_ROW_TILE = 512
FFN_HIDDEN_SPLITS = ((0, 1536), (1536, FFN_HIDDEN))


def _bf16(x):
    return x.astype(jnp.bfloat16)


def _dot(a, b):
    return jnp.dot(a, b, preferred_element_type=jnp.float32)


def _dot_nt(a, b):
    return lax.dot_general(a, b, (((1,), (1,)), ((), ())), preferred_element_type=jnp.float32)


def _dot_tn(a, b):
    return lax.dot_general(a, b, (((0,), (0,)), ((), ())), preferred_element_type=jnp.float32)


def _rms_rows(x):
    return x * lax.rsqrt(jnp.mean(x * x, axis=-1, keepdims=True) + EPS)


def _const_spec(shape):
    nd = len(shape)
    return pl.BlockSpec(shape, lambda *_: (0,) * nd, pipeline_mode=pl.Buffered(1))


def _params(sem):
    return pltpu.CompilerParams(dimension_semantics=sem, vmem_limit_bytes=VMEM_LIMIT)


def _inproj_kernel(x_ref, g_ref, w_ref, wg_ref, bg_ref, q_ref, k_ref, v_ref, o_ref, gate_ref):
    hn = _bf16(_rms_rows(x_ref[...]) * g_ref[...])
    c0, c1, c2, c3 = ML_QK_COLS, 2 * ML_QK_COLS, 2 * ML_QK_COLS + ML_V_COLS, 2 * ML_QK_COLS + 2 * ML_V_COLS
    q_ref[...] = _bf16(_dot(hn, w_ref[:, 0:c0]) * (ML_DQK ** -0.5))
    k_ref[...] = _bf16(_dot(hn, w_ref[:, c0:c1]))
    v_ref[...] = _bf16(_dot(hn, w_ref[:, c1:c2]))
    o_ref[...] = _bf16(_dot(hn, w_ref[:, c2:c3]))

    pre = _dot(hn, wg_ref[...]) + bg_ref[...]
    cap = GATE_CAP * jnp.tanh(pre * (1.0 / GATE_CAP))
    logf = jnp.minimum(cap, 0.0) - jnp.log1p(jnp.exp(-jnp.abs(cap)))
    rows = cap.shape[0]
    row_in_chunk = lax.broadcasted_iota(jnp.int32, cap.shape, 0) % ML_CHUNK
    cum = logf
    d = 1
    while d < ML_CHUNK:
        cum = cum + jnp.where(row_in_chunk >= d, pltpu.roll(cum, d, 0), 0.0)
        d *= 2
    li_shift = pltpu.roll(cap, ML_HEADS, 1)
    parts = []
    for c in range(rows // ML_CHUNK):
        sl = slice(c * ML_CHUNK, (c + 1) * ML_CHUNK)
        g_last = cum[(c + 1) * ML_CHUNK - 1:(c + 1) * ML_CHUNK, :]
        parts.append(g_last - cum[sl] + li_shift[sl])
    wlog = pltpu.roll(jnp.concatenate(parts, axis=0), ML_HEADS, 1)
    lane = lax.broadcasted_iota(jnp.int32, cap.shape, 1)
    gate_ref[...] = jnp.where(lane < ML_HEADS, cap, jnp.where(lane < 2 * ML_HEADS, cum, wlog))


def _inproj(x2, g, w, wg, bg):
    t = x2.shape[0]
    tm = ROW_TILE
    row = lambda n: pl.BlockSpec((tm, n), lambda i: (i, 0))
    return pl.pallas_call(
        _inproj_kernel,
        grid=(t // tm,),
        in_specs=[row(D_MODEL), _const_spec((1, D_MODEL)), _const_spec(w.shape), _const_spec(wg.shape),
                  _const_spec((1, GATE_LANES))],
        out_specs=[row(ML_QK_COLS), row(ML_QK_COLS), row(ML_V_COLS), row(ML_V_COLS), row(GATE_LANES)],
        out_shape=[jax.ShapeDtypeStruct((t, ML_QK_COLS), jnp.bfloat16),
                   jax.ShapeDtypeStruct((t, ML_QK_COLS), jnp.bfloat16),
                   jax.ShapeDtypeStruct((t, ML_V_COLS), jnp.bfloat16),
                   jax.ShapeDtypeStruct((t, ML_V_COLS), jnp.bfloat16),
                   jax.ShapeDtypeStruct((t, GATE_LANES), jnp.float32)],
        compiler_params=_params(("parallel",)),
        name="inproj",
    )(x2, g, w, wg, bg)


def _mlstm_kernel(q_ref, k_ref, v_ref, o_ref, gcol_ref, grow_ref, gain_ref, y_ref, c_ref):
    L = ML_CHUNK

    @pl.when(pl.program_id(1) == 0)
    def _():
        c_ref[...] = jnp.zeros_like(c_ref)

    lane_row = lax.broadcasted_iota(jnp.int32, (1, LANES), 1)
    r_i = lax.broadcasted_iota(jnp.int32, (L, L), 0)
    c_i = lax.broadcasted_iota(jnp.int32, (L, L), 1)
    causal = r_i >= c_i
    gc = gcol_ref[...]
    egc = jnp.exp(gc)
    gain = gain_ref[...]
    heads = range(ML_HEADS)
    pair = lambda h: slice((h // 2) * LANES, (h // 2 + 1) * LANES)
    cols = lambda h: slice(h * ML_DV, (h + 1) * ML_DV)
    col = lambda a, c: a[:, c:c + 1]

    ones = jnp.ones((L, ML_DV), jnp.bfloat16)
    vs = [jnp.concatenate([v_ref[:, cols(h)], ones], axis=1) for h in heads]

    scores, inter = [], []
    for h in heads:
        head_lanes = (lane_row // ML_DQK) == (h % 2)
        q = jnp.where(head_lanes, q_ref[:, pair(h)], jnp.zeros((), jnp.bfloat16))
        scores.append(_dot_nt(q, k_ref[:, pair(h)]))
        inter.append(_dot(q, _bf16(c_ref[h])))

    ps = []
    for h in heads:
        c_row = grow_ref[0, h:h + 1, :] - grow_ref[0, ML_HEADS + h:ML_HEADS + h + 1, :]
        dmat = jnp.where(causal, col(gc, ML_HEADS + h) + c_row, NEG_BIG)
        ps.append(_bf16(scores[h] * jnp.exp(dmat)))

    for h in heads:
        both = _dot(ps[h], vs[h]) + inter[h] * col(egc, ML_HEADS + h)
        hc = both[:, :ML_DV] / jnp.maximum(jnp.abs(both[:, ML_DV:]), 1.0)
        hn = _rms_rows(hc) * gain
        y_ref[:, cols(h)] = _bf16(jax.nn.sigmoid(o_ref[:, cols(h)].astype(jnp.float32)) * hn)

    for h in heads:
        decay = jnp.exp(grow_ref[0, ML_HEADS + h:ML_HEADS + h + 1, L - 1:L])
        kw = k_ref[:, pair(h)].astype(jnp.float32) * col(egc, 2 * ML_HEADS + h)
        c_ref[h] = decay * c_ref[h] + _dot_tn(_bf16(kw), vs[h])


def _mlstm(q, k, v, o, gcol, grow, gain, batch, seq):
    L = ML_CHUNK
    nc = seq // L
    t = batch * seq
    row = lambda n: pl.BlockSpec((L, n), lambda b, c: (b * nc + c, 0))
    return pl.pallas_call(
        _mlstm_kernel,
        grid=(batch, nc),
        in_specs=[row(ML_QK_COLS), row(ML_QK_COLS), row(ML_V_COLS), row(ML_V_COLS), row(GATE_LANES),
                  pl.BlockSpec((1, 3 * ML_HEADS, L), lambda b, c: (b, 0, c)),
                  pl.BlockSpec((1, ML_DV), lambda b, c: (0, 0))],
        out_specs=row(ML_V_COLS),
        out_shape=jax.ShapeDtypeStruct((t, ML_V_COLS), jnp.bfloat16),
        scratch_shapes=[pltpu.VMEM((ML_HEADS, LANES, 2 * ML_DV), jnp.float32)],
        compiler_params=_params(("parallel", "arbitrary")),
        name="mlstm",
    )(q, k, v, o, gcol, grow, gain)


def _post_kernel(h_ref, y_ref, p_ref, wo_ref, gf_ref, wgu_ref, wd_ref, gp_ref, wpg_ref, wpu_ref, out_ref):
    h1 = h_ref[...] + _dot(y_ref[...], wo_ref[...])
    hn = _bf16(_rms_rows(h1) * gf_ref[...])
    h2 = h1
    for lo, hi in FFN_HIDDEN_SPLITS:
        gate_pre = _dot(hn, wgu_ref[:, lo:hi])
        up_pre = _dot(hn, wgu_ref[:, FFN_HIDDEN + lo:FFN_HIDDEN + hi])
        h2 = h2 + _dot(_bf16(jax.nn.silu(gate_pre) * up_pre), wd_ref[lo:hi, :])
    hp = _bf16(_rms_rows(h2) * gp_ref[...])
    gate = jax.nn.sigmoid(_dot(hp, wpg_ref[...]))
    up = _dot(_bf16(p_ref[...]), wpu_ref[...])
    out_ref[...] = h2 + up * gate


def _post(h, y, p, layer, wo, gf, wgu, wd, gp, wpg, wpu):
    t = h.shape[0]
    tm = POST_ROW_TILE
    row = lambda n: pl.BlockSpec((tm, n), lambda i: (i, 0))
    p_spec = pl.BlockSpec((None, tm, PLE_DIM), lambda i: (layer, i, 0))
    return pl.pallas_call(
        _post_kernel,
        grid=(t // tm,),
        in_specs=[row(D_MODEL), row(D_MODEL), p_spec, _const_spec(wo.shape), _const_spec((1, D_MODEL)),
                  _const_spec(wgu.shape), _const_spec(wd.shape), _const_spec((1, D_MODEL)),
                  _const_spec(wpg.shape), _const_spec(wpu.shape)],
        out_specs=row(D_MODEL),
        out_shape=jax.ShapeDtypeStruct((t, D_MODEL), jnp.float32),
        compiler_params=_params(("parallel",)),
        name="post",
    )(h, y, p, wo, gf, wgu, wd, gp, wpg, wpu)


def _head_mean_square(x, ones_bd):
    sq = x * x
    hi = _bf16(sq)
    lo = _bf16(sq - hi.astype(jnp.float32))
    return (_dot(hi, ones_bd) + _dot(lo, ones_bd)) * (1.0 / MB_HD)


def _norm_rope(x, ms, gain, cos_t, sin_lo, sin_hi):
    xn = x * lax.rsqrt(ms + EPS) * gain
    outs = []
    for s in range(x.shape[1] // LANES):
        xs = xn[:, s * LANES:(s + 1) * LANES]
        outs.append(xs * cos_t + pltpu.roll(xs, LANES - ROPE_HALF, 1) * sin_lo + pltpu.roll(xs, ROPE_HALF, 1) * sin_hi)
    return jnp.concatenate(outs, axis=1)


def _proj_kernel(h_ref, gq_ref, gkv_ref, wq_ref, wk_ref, wv_ref, qn_ref, kn_ref, bd_ref, cos_ref, slo_ref, shi_ref,
                 q_ref, k_ref, vt_ref, km_ref):
    r = _rms_rows(h_ref[...])
    hq = _bf16(r * gq_ref[...])
    hkv = _bf16(r * gkv_ref[...])
    tables = (cos_ref[...], slo_ref[...], shi_ref[...])
    bd = bd_ref[...]
    kvw = MB_KV_HEADS * MB_HD
    slab = lambda s: slice(s * kvw, (s + 1) * kvw)
    nq = D_MODEL // kvw
    raw = [_dot(hq, wq_ref[:, slab(s)]) for s in range(nq)] + [_dot(hkv, wk_ref[:, slab(s)]) for s in range(2)]
    gains = [qn_ref[:, slab(s)] for s in range(nq)] + [kn_ref[:, slab(s)] for s in range(2)]
    vs = _dot(hkv, wv_ref[...])
    means = [_head_mean_square(x, bd) for x in raw]
    vt_ref[0] = _bf16(vs.T)
    for s in range(nq):
        q_ref[:, slab(s)] = _bf16(_norm_rope(raw[s], means[s], gains[s], *tables))
    for s in range(2):
        ks = _norm_rope(raw[nq + s], means[nq + s], gains[nq + s], *tables)
        k_ref[:, slab(s)] = _bf16(ks)
        for b in range(ks.shape[0] // MB_BLOCK):
            km_ref[b, :, slab(s)] = jnp.mean(ks[b * MB_BLOCK:(b + 1) * MB_BLOCK], axis=0, keepdims=True)


def _proj(h, gq, gkv, wq, wk2, wv, qn, kn2, bd, cos_t, sin_lo, sin_hi, seq):
    t = h.shape[0]
    tm = ROW_TILE
    kvw = MB_KV_HEADS * MB_HD
    row = lambda n: pl.BlockSpec((tm, n), lambda i: (i, 0))
    tab = pl.BlockSpec((tm, LANES), lambda i: (i % (seq // tm), 0))
    return pl.pallas_call(
        _proj_kernel,
        grid=(t // tm,),
        in_specs=[row(D_MODEL), _const_spec((1, D_MODEL)), _const_spec((1, D_MODEL)), _const_spec(wq.shape),
                  _const_spec(wk2.shape), _const_spec(wv.shape), _const_spec((1, D_MODEL)), _const_spec((1, 2 * kvw)),
                  _const_spec(bd.shape), tab, tab, tab],
        out_specs=[row(D_MODEL), row(2 * kvw),
                   pl.BlockSpec((1, kvw, tm), lambda i: (i // (seq // tm), 0, i % (seq // tm))),
                   pl.BlockSpec((tm // MB_BLOCK, 1, 2 * kvw), lambda i: (i, 0, 0))],
        out_shape=[jax.ShapeDtypeStruct((t, D_MODEL), jnp.bfloat16),
                   jax.ShapeDtypeStruct((t, 2 * kvw), jnp.bfloat16),
                   jax.ShapeDtypeStruct((t // seq, kvw, seq), jnp.bfloat16),
                   jax.ShapeDtypeStruct((t // MB_BLOCK, 1, 2 * kvw), jnp.float32)],
        compiler_params=_params(("parallel",)),
        name="proj",
    )(h, gq, gkv, wq, wk2, wv, qn, kn2, bd, cos_t, sin_lo, sin_hi)


MB_GROUPS_PER_STEP = 2


def _moba_kernel(*refs, i):
    q_ref, k_ref, vt_ref, km_ref = refs[:4]
    o_ref, qa_ref, s_ref, acc_ref, vaug_ref = refs[-5:]
    nb = km_ref.shape[1]
    gw = MB_GROUP * MB_HD
    streams = [(gg, h) for gg in range(MB_GROUPS_PER_STEP) for h in range(MB_GROUP)]
    sid = lambda gg, h: gg * MB_GROUP + h
    lanes = lambda gg: slice(gg * LANES, (gg + 1) * LANES)
    nkeys = (i + 1) * MB_BLOCK
    depth = s_ref.shape[1]
    first_half = lax.broadcasted_iota(jnp.int32, (1, LANES), 1) < MB_HD
    zero = jnp.zeros((), jnp.bfloat16)
    for gg in range(MB_GROUPS_PER_STEP):
        for u in range(MB_GROUP // 2):
            t = q_ref[:, gg * gw + u * LANES:gg * gw + (u + 1) * LANES]
            qa_ref[sid(gg, 2 * u)] = jnp.where(first_half, t, zero)
            qa_ref[sid(gg, 2 * u + 1)] = jnp.where(first_half, zero, t)
        vaug_ref[gg, :MB_HD, :] = vt_ref[0, gg * MB_HD:(gg + 1) * MB_HD, :]
        vaug_ref[gg, MB_HD:, :] = jnp.ones((vaug_ref.shape[1] - MB_HD, nkeys), jnp.bfloat16)

    blk = lax.broadcasted_iota(jnp.int32, (nb, MB_BLOCK), 0)
    past = blk < i
    key_i = lax.broadcasted_iota(jnp.int32, (MB_BLOCK, MB_BLOCK), 0)
    qry_i = lax.broadcasted_iota(jnp.int32, (MB_BLOCK, MB_BLOCK), 1)
    causal = key_i <= qry_i

    def scores_of(n):
        maxes = {}
        for gg, h in streams:
            s = _dot_nt(k_ref[0, n, :, lanes(gg)], qa_ref[sid(gg, h)])
            s_ref[sid(gg, h), n % depth] = s
            maxes[gg, h] = jnp.max(s, axis=0, keepdims=True)
        return maxes

    def weights(s, offset):
        return jnp.exp2(_bf16(s - offset))

    own, chosen = {}, {}
    for gg in range(MB_GROUPS_PER_STEP):
        km = km_ref[0, :, 0, lanes(gg)]
        km_hi = _bf16(km)
        km_lo = _bf16(km - km_hi.astype(jnp.float32))
        own_and_means = jnp.concatenate([k_ref[0, i, :, lanes(gg)], km_hi, km_lo], axis=0)
        for h in range(MB_GROUP):
            r = _dot_nt(own_and_means, qa_ref[sid(gg, h)])
            own[gg, h] = jnp.where(causal, r[:MB_BLOCK], NEG_BIG)
            g = jnp.where(past, r[MB_BLOCK:MB_BLOCK + nb] + r[MB_BLOCK + nb:], -jnp.inf)
            picked = jnp.zeros((nb, MB_BLOCK), jnp.bool_)
            for _ in range(MB_TOPK):
                top = jnp.max(g, axis=0, keepdims=True)
                first = jnp.min(jnp.where(g == top, blk, nb), axis=0, keepdims=True)
                pick = blk == first
                picked = picked | pick
                g = jnp.where(pick, -jnp.inf, g)
            chosen[gg, h] = picked & past
    block_max = scores_of(0) if i > 0 else None

    ms, ls = {}, {}
    acc_rows = lambda gg, h: slice(sid(gg, h) * MB_HD, (sid(gg, h) + 1) * MB_HD)
    for gg, h in streams:
        mx = jnp.max(own[gg, h], axis=0, keepdims=True)
        pv = _dot(vaug_ref[gg, :, i * MB_BLOCK:nkeys], weights(own[gg, h], mx))
        ms[gg, h] = mx
        ls[gg, h] = pv[MB_HD:MB_HD + 1, :]
        acc_ref[acc_rows(gg, h), :] = pv[:MB_HD, :]

    for n in range(i):
        next_max = scores_of(n + 1) if n + 1 < i else None
        for gg, h in streams:
            use = chosen[gg, h][n:n + 1]
            m_new = jnp.maximum(ms[gg, h], jnp.where(use, block_max[gg, h], NEG_BIG))
            alpha = jnp.exp2(ms[gg, h] - m_new)
            p = weights(s_ref[sid(gg, h), n % depth], jnp.where(use, m_new, -NEG_BIG))
            pv = _dot(vaug_ref[gg, :, n * MB_BLOCK:(n + 1) * MB_BLOCK], p)
            ms[gg, h] = m_new
            ls[gg, h] = alpha * ls[gg, h] + pv[MB_HD:MB_HD + 1, :]
            acc_ref[acc_rows(gg, h), :] = alpha * acc_ref[acc_rows(gg, h), :] + pv[:MB_HD, :]
        block_max = next_max

    for gg, h in streams:
        acc_ref[acc_rows(gg, h), :] = acc_ref[acc_rows(gg, h), :] / ls[gg, h]
    for gg in range(MB_GROUPS_PER_STEP):
        o_ref[:, gg * gw:(gg + 1) * gw] = _bf16(acc_ref[gg * gw:(gg + 1) * gw, :].T)


def _moba(q, k, vt, km, batch, seq):
    nb = seq // MB_BLOCK
    t = batch * seq
    gs = MB_GROUPS_PER_STEP
    gw = gs * MB_GROUP * MB_HD
    n_streams = gs * MB_GROUP
    y = None
    for i in range(nb):
        prev = [] if y is None else [y]
        nkeys = (i + 1) * MB_BLOCK
        y = pl.pallas_call(
            functools.partial(_moba_kernel, i=i),
            grid=(batch, MB_KV_HEADS // gs),
            in_specs=[pl.BlockSpec((MB_BLOCK, gw), lambda b, g, i=i: (b * nb + i, g)),
                      pl.BlockSpec((1, i + 1, MB_BLOCK, gs * LANES), lambda b, g: (b, 0, 0, g)),
                      pl.BlockSpec((1, gs * MB_HD, nkeys), lambda b, g: (b, g, 0)),
                      pl.BlockSpec((1, nb, 1, gs * LANES), lambda b, g: (b, 0, 0, g)),
                      ] + [pl.BlockSpec(memory_space=pl.ANY)] * len(prev),
            out_specs=pl.BlockSpec((MB_BLOCK, gw), lambda b, g, i=i: (b * nb + i, g)),
            out_shape=jax.ShapeDtypeStruct((t, MB_HEADS * MB_HD), jnp.bfloat16),
            input_output_aliases={4: 0} if prev else {},
            scratch_shapes=[pltpu.VMEM((n_streams, MB_BLOCK, LANES), jnp.bfloat16),
                            pltpu.VMEM((n_streams, min(max(i, 1), 3), MB_BLOCK, MB_BLOCK), jnp.float32),
                            pltpu.VMEM((n_streams * MB_HD, MB_BLOCK), jnp.float32),
                            pltpu.VMEM((gs, MB_HD + 16, nkeys), jnp.bfloat16)],
            compiler_params=_params(("parallel", "parallel")),
            name=f"moba{i}",
        )(q, k, vt, km, *prev)
    return y


def _rope_tables(seq):
    inv = ROPE_THETA ** (-np.arange(0, ROPE_DIM, 2, dtype=np.float64) / ROPE_DIM)
    ang = np.arange(seq, dtype=np.float64)[:, None] * inv[None, :]
    cos, sin = np.cos(ang), np.sin(ang)
    zeros = np.zeros((seq, MB_HD - ROPE_DIM))
    zh = np.zeros((seq, ROPE_HALF))
    cos_h = np.concatenate([cos, cos, np.ones_like(zeros)], axis=1)
    lo_h = np.concatenate([-sin, zh, zeros], axis=1)
    hi_h = np.concatenate([zh, sin, zeros], axis=1)
    two = lambda a: jnp.asarray(np.concatenate([a, a], axis=1), jnp.float32)
    return two(cos_h), two(lo_h), two(hi_h)


def kernel(x, p, norm_mix, a_w_in, a_b_gate, a_mh_gain, a_w_out, kv_norm, w_kv, k_norm, b_w_q, b_q_norm, b_w_o,
           norm_ffn, w_gate_up, w_down, norm_ple, w_ple_gate, w_ple_up):
    batch, seq, d = x.shape
    t = batch * seq
    f32 = jnp.float32
    x2 = x.reshape(t, d)
    p2 = p.reshape(p.shape[0], t, PLE_DIM)
    row = lambda a: a.reshape(1, -1).astype(f32)

    w_in = a_w_in[0]
    n_main = 2 * ML_QK_COLS + 2 * ML_V_COLS
    w_main = _bf16(w_in[:, :n_main])
    w_gate = _bf16(jnp.pad(w_in[:, n_main:], ((0, 0), (0, GATE_LANES - 2 * ML_HEADS))))
    b_gate = jnp.pad(a_b_gate[0].astype(f32), (0, GATE_LANES - 2 * ML_HEADS)).reshape(1, GATE_LANES)
    q, k, v, o, gcol = _inproj(x2, row(norm_mix[0]), w_main, w_gate, b_gate)
    grow = jnp.transpose(gcol.reshape(batch, seq, GATE_LANES)[:, :, :3 * ML_HEADS], (0, 2, 1))
    y = _mlstm(q, k, v, o, gcol, grow, row(a_mh_gain[0]), batch, seq)
    h = _post(x2, y, p2, 0, _bf16(a_w_out[0]), row(norm_ffn[0]), _bf16(w_gate_up[0]), _bf16(w_down[0]),
              row(norm_ple[0]), _bf16(w_ple_gate[0]), _bf16(w_ple_up[0]))

    nb = seq // MB_BLOCK
    kvw = MB_KV_HEADS * MB_HD
    cos_t, sin_lo, sin_hi = _rope_tables(seq)
    head = lax.broadcasted_iota(jnp.int32, (kvw, kvw), 0) // MB_HD
    ones_bd = (head == head.T).astype(jnp.bfloat16)
    qn = row(jnp.tile(b_q_norm[0], MB_HEADS)) * MB_Q_SCALE
    kn2 = row(jnp.tile(k_norm, 2 * MB_KV_HEADS))
    w_k = _bf16(w_kv[:, :kvw]).reshape(d, MB_KV_HEADS, 1, MB_HD)
    w_k2 = jnp.broadcast_to(w_k, (d, MB_KV_HEADS, 2, MB_HD)).reshape(d, 2 * kvw)
    q, k2, vt, km2 = _proj(h, row(norm_mix[1]), row(kv_norm), _bf16(b_w_q[0]), w_k2, _bf16(w_kv[:, kvw:]), qn, kn2,
                           ones_bd, cos_t, sin_lo, sin_hi, seq)
    y = _moba(q, k2.reshape(batch, nb, MB_BLOCK, 2 * kvw), vt, km2.reshape(batch, nb, 1, 2 * kvw), batch, seq)
    h = _post(h, y, p2, 1, _bf16(b_w_o[0]), row(norm_ffn[1]), _bf16(w_gate_up[1]), _bf16(w_down[1]),
              row(norm_ple[1]), _bf16(w_ple_gate[1]), _bf16(w_ple_up[1]))
    return h.reshape(batch, seq, d)
```

```python
import functools
import math

import jax
import jax.numpy as jnp
from jax import lax
from jax.experimental import pallas as pl
from jax.experimental.pallas import tpu as pltpu
import numpy as np

D_MODEL = 1024
PLE_DIM = 256
EPS = 1e-6

ML_HEADS = 8
ML_DV = 128
ML_DQK = 64
GATE_CAP = 15.0
ML_QK_COLS = ML_HEADS * ML_DQK
ML_V_COLS = ML_HEADS * ML_DV
ML_CHUNK = 256

MB_HEADS = 16
MB_KV_HEADS = 4
MB_HD = 64
MB_GROUP = MB_HEADS // MB_KV_HEADS
MB_BLOCK = 256
MB_TOPK = 3
ROPE_THETA = 500000.0
ROPE_DIM = MB_HD // 4
ROPE_HALF = ROPE_DIM // 2
MB_Q_SCALE = MB_HD ** -0.5 * math.log2(math.e)

FFN_HIDDEN = 2816

LANES = 128
GATE_LANES = LANES
NEG_BIG = -1e30
VMEM_LIMIT = 56 * 1024 * 1024

ROW_TILE = 512
POST_ROW_TILE = 512
FFN_HIDDEN_SPLITS = ((0, 1536), (1536, FFN_HIDDEN))


def _bf16(x):
    return x.astype(jnp.bfloat16)


def _dot(a, b):
    return jnp.dot(a, b, preferred_element_type=jnp.float32)


def _dot_nt(a, b):
    return lax.dot_general(a, b, (((1,), (1,)), ((), ())), preferred_element_type=jnp.float32)


def _dot_tn(a, b):
    return lax.dot_general(a, b, (((0,), (0,)), ((), ())), preferred_element_type=jnp.float32)


def _rms_rows(x):
    return x * lax.rsqrt(jnp.mean(x * x, axis=-1, keepdims=True) + EPS)


def _const_spec(shape):
    nd = len(shape)
    return pl.BlockSpec(shape, lambda *_: (0,) * nd, pipeline_mode=pl.Buffered(1))


def _params(sem):
    return pltpu.CompilerParams(dimension_semantics=sem, vmem_limit_bytes=VMEM_LIMIT)


def _inproj_kernel(x_ref, g_ref, w_ref, wg_ref, bg_ref, q_ref, k_ref, v_ref, o_ref, gate_ref):
    hn = _bf16(_rms_rows(x_ref[...]) * g_ref[...])
    c0, c1, c2, c3 = ML_QK_COLS, 2 * ML_QK_COLS, 2 * ML_QK_COLS + ML_V_COLS, 2 * ML_QK_COLS + 2 * ML_V_COLS
    pre = _dot(hn, wg_ref[...]) + bg_ref[...]
    cap = GATE_CAP * jnp.tanh(pre * (1.0 / GATE_CAP))
    logf = jnp.minimum(cap, 0.0) - jnp.log1p(jnp.exp(-jnp.abs(cap)))
    rows = cap.shape[0]
    row_in_chunk = lax.broadcasted_iota(jnp.int32, cap.shape, 0) % ML_CHUNK
    cum = logf
    d = 1
    while d < ML_CHUNK:
        cum = cum + jnp.where(row_in_chunk >= d, pltpu.roll(cum, d, 0), 0.0)
        d *= 2
    li_shift = pltpu.roll(cap, ML_HEADS, 1)
    parts = []
    for c in range(rows // ML_CHUNK):
        sl = slice(c * ML_CHUNK, (c + 1) * ML_CHUNK)
        g_last = cum[(c + 1) * ML_CHUNK - 1:(c + 1) * ML_CHUNK, :]
        parts.append(g_last - cum[sl] + li_shift[sl])
    wlog = pltpu.roll(jnp.concatenate(parts, axis=0), ML_HEADS, 1)
    lane = lax.broadcasted_iota(jnp.int32, cap.shape, 1)
    gate_ref[...] = jnp.where(lane < ML_HEADS, cap, jnp.where(lane < 2 * ML_HEADS, cum, wlog))

    q_ref[...] = _bf16(_dot(hn, w_ref[:, 0:c0]) * (ML_DQK ** -0.5))
    k_ref[...] = _bf16(_dot(hn, w_ref[:, c0:c1]))
    v_ref[...] = _bf16(_dot(hn, w_ref[:, c1:c2]))
    o_ref[...] = _bf16(_dot(hn, w_ref[:, c2:c3]))


def _inproj(x2, g, w, wg, bg):
    t = x2.shape[0]
    tm = ROW_TILE
    row = lambda n: pl.BlockSpec((tm, n), lambda i: (i, 0))
    return pl.pallas_call(
        _inproj_kernel,
        grid=(t // tm,),
        in_specs=[row(D_MODEL), _const_spec((1, D_MODEL)), _const_spec(w.shape), _const_spec(wg.shape),
                  _const_spec((1, GATE_LANES))],
        out_specs=[row(ML_QK_COLS), row(ML_QK_COLS), row(ML_V_COLS), row(ML_V_COLS), row(GATE_LANES)],
        out_shape=[jax.ShapeDtypeStruct((t, ML_QK_COLS), jnp.bfloat16),
                   jax.ShapeDtypeStruct((t, ML_QK_COLS), jnp.bfloat16),
                   jax.ShapeDtypeStruct((t, ML_V_COLS), jnp.bfloat16),
                   jax.ShapeDtypeStruct((t, ML_V_COLS), jnp.bfloat16),
                   jax.ShapeDtypeStruct((t, GATE_LANES), jnp.float32)],
        compiler_params=_params(("parallel",)),
        name="inproj",
    )(x2, g, w, wg, bg)


def _mlstm_kernel(q_ref, k_ref, v_ref, o_ref, gcol_ref, grow_ref, gain_ref, y_ref, c_ref):
    L = ML_CHUNK

    @pl.when(pl.program_id(1) == 0)
    def _():
        c_ref[...] = jnp.zeros_like(c_ref)

    lane_row = lax.broadcasted_iota(jnp.int32, (1, LANES), 1)
    r_i = lax.broadcasted_iota(jnp.int32, (L, L), 0)
    c_i = lax.broadcasted_iota(jnp.int32, (L, L), 1)
    causal = r_i >= c_i
    gc = gcol_ref[...]
    egc = jnp.exp(gc)
    gain = gain_ref[...]
    heads = range(ML_HEADS)
    pair = lambda h: slice((h // 2) * LANES, (h // 2 + 1) * LANES)
    cols = lambda h: slice(h * ML_DV, (h + 1) * ML_DV)
    col = lambda a, c: a[:, c:c + 1]

    ones = jnp.ones((L, ML_DV), jnp.bfloat16)
    vs = [jnp.concatenate([v_ref[:, cols(h)], ones], axis=1) for h in heads]

    scores, inter = [], []
    for h in heads:
        head_lanes = (lane_row // ML_DQK) == (h % 2)
        q = jnp.where(head_lanes, q_ref[:, pair(h)], jnp.zeros((), jnp.bfloat16))
        scores.append(_dot_nt(q, k_ref[:, pair(h)]))
        inter.append(_dot(q, _bf16(c_ref[h])))

    ps = []
    for h in heads:
        c_row = grow_ref[0, h:h + 1, :] - grow_ref[0, ML_HEADS + h:ML_HEADS + h + 1, :]
        dmat = jnp.where(causal, col(gc, ML_HEADS + h) + c_row, NEG_BIG)
        ps.append(_bf16(scores[h] * jnp.exp(dmat)))

    for h in heads:
        both = _dot(ps[h], vs[h]) + inter[h] * col(egc, ML_HEADS + h)
        hc = both[:, :ML_DV] / jnp.maximum(jnp.abs(both[:, ML_DV:]), 1.0)
        hn = _rms_rows(hc) * gain
        y_ref[:, cols(h)] = _bf16(jax.nn.sigmoid(o_ref[:, cols(h)].astype(jnp.float32)) * hn)

    for h in heads:
        decay = jnp.exp(grow_ref[0, ML_HEADS + h:ML_HEADS + h + 1, L - 1:L])
        kw = k_ref[:, pair(h)].astype(jnp.float32) * col(egc, 2 * ML_HEADS + h)
        c_ref[h] = decay * c_ref[h] + _dot_tn(_bf16(kw), vs[h])


def _mlstm(q, k, v, o, gcol, grow, gain, batch, seq):
    L = ML_CHUNK
    nc = seq // L
    t = batch * seq
    row = lambda n: pl.BlockSpec((L, n), lambda b, c: (b * nc + c, 0))
    return pl.pallas_call(
        _mlstm_kernel,
        grid=(batch, nc),
        in_specs=[row(ML_QK_COLS), row(ML_QK_COLS), row(ML_V_COLS), row(ML_V_COLS), row(GATE_LANES),
                  pl.BlockSpec((1, 3 * ML_HEADS, L), lambda b, c: (b, 0, c)),
                  pl.BlockSpec((1, ML_DV), lambda b, c: (0, 0))],
        out_specs=row(ML_V_COLS),
        out_shape=jax.ShapeDtypeStruct((t, ML_V_COLS), jnp.bfloat16),
        scratch_shapes=[pltpu.VMEM((ML_HEADS, LANES, 2 * ML_DV), jnp.float32)],
        compiler_params=_params(("parallel", "arbitrary")),
        name="mlstm",
    )(q, k, v, o, gcol, grow, gain)


def _post_kernel(h_ref, y_ref, p_ref, wo_ref, gf_ref, wgu_ref, wd_ref, gp_ref, wpg_ref, wpu_ref, out_ref):
    h1 = h_ref[...] + _dot(y_ref[...], wo_ref[...])
    hn = _bf16(_rms_rows(h1) * gf_ref[...])
    h2 = h1
    for lo, hi in FFN_HIDDEN_SPLITS:
        gate_pre = _dot(hn, wgu_ref[:, lo:hi])
        up_pre = _dot(hn, wgu_ref[:, FFN_HIDDEN + lo:FFN_HIDDEN + hi])
        h2 = h2 + _dot(_bf16(jax.nn.silu(gate_pre) * up_pre), wd_ref[lo:hi, :])
    hp = _bf16(_rms_rows(h2) * gp_ref[...])
    gate = jax.nn.sigmoid(_dot(hp, wpg_ref[...]))
    up = _dot(_bf16(p_ref[...]), wpu_ref[...])
    out_ref[...] = h2 + up * gate


def _post(h, y, p, layer, wo, gf, wgu, wd, gp, wpg, wpu):
    t = h.shape[0]
    tm = POST_ROW_TILE
    row = lambda n: pl.BlockSpec((tm, n), lambda i: (i, 0))
    p_spec = pl.BlockSpec((None, tm, PLE_DIM), lambda i: (layer, i, 0))
    return pl.pallas_call(
        _post_kernel,
        grid=(t // tm,),
        in_specs=[row(D_MODEL), row(D_MODEL), p_spec, _const_spec(wo.shape), _const_spec((1, D_MODEL)),
                  _const_spec(wgu.shape), _const_spec(wd.shape), _const_spec((1, D_MODEL)),
                  _const_spec(wpg.shape), _const_spec(wpu.shape)],
        out_specs=row(D_MODEL),
        out_shape=jax.ShapeDtypeStruct((t, D_MODEL), jnp.float32),
        compiler_params=_params(("parallel",)),
        name="post",
    )(h, y, p, wo, gf, wgu, wd, gp, wpg, wpu)


def _head_mean_square(x, ones_bd):
    return _dot(_bf16(x * x), ones_bd) * (1.0 / MB_HD)


def _norm_rope(x, ms, gain, cos_t, sin_lo, sin_hi):
    xn = x * lax.rsqrt(ms + EPS) * gain
    outs = []
    for s in range(x.shape[1] // LANES):
        xs = xn[:, s * LANES:(s + 1) * LANES]
        outs.append(xs * cos_t + pltpu.roll(xs, LANES - ROPE_HALF, 1) * sin_lo + pltpu.roll(xs, ROPE_HALF, 1) * sin_hi)
    return jnp.concatenate(outs, axis=1)


def _proj_kernel(h_ref, gq_ref, gkv_ref, wq_ref, wk_ref, wv_ref, qn_ref, kn_ref, bd_ref, cos_ref, slo_ref, shi_ref,
                 q_ref, k_ref, vt_ref, km_ref):
    r = _rms_rows(h_ref[...])
    hq = _bf16(r * gq_ref[...])
    hkv = _bf16(r * gkv_ref[...])
    tables = (cos_ref[...], slo_ref[...], shi_ref[...])
    bd = bd_ref[...]
    kvw = MB_KV_HEADS * MB_HD
    slab = lambda s: slice(s * kvw, (s + 1) * kvw)
    nq = D_MODEL // kvw
    raw = [_dot(hq, wq_ref[:, slab(s)]) for s in range(nq)] + [_dot(hkv, wk_ref[:, slab(s)]) for s in range(2)]
    gains = [qn_ref[:, slab(s)] for s in range(nq)] + [kn_ref[:, slab(s)] for s in range(2)]
    vs = _dot(hkv, wv_ref[...])
    means = [_head_mean_square(x, bd) for x in raw]
    vt_ref[0] = _bf16(vs.T)
    for s in range(nq):
        q_ref[:, slab(s)] = _bf16(_norm_rope(raw[s], means[s], gains[s], *tables))
    for s in range(2):
        ks = _norm_rope(raw[nq + s], means[nq + s], gains[nq + s], *tables)
        k_ref[:, slab(s)] = _bf16(ks)
        for b in range(ks.shape[0] // MB_BLOCK):
            km_ref[b, :, slab(s)] = jnp.mean(ks[b * MB_BLOCK:(b + 1) * MB_BLOCK], axis=0, keepdims=True)


def _proj(h, gq, gkv, wq, wk2, wv, qn, kn2, bd, cos_t, sin_lo, sin_hi, seq):
    t = h.shape[0]
    tm = ROW_TILE
    kvw = MB_KV_HEADS * MB_HD
    row = lambda n: pl.BlockSpec((tm, n), lambda i: (i, 0))
    tab = pl.BlockSpec((tm, LANES), lambda i: (i % (seq // tm), 0))
    return pl.pallas_call(
        _proj_kernel,
        grid=(t // tm,),
        in_specs=[row(D_MODEL), _const_spec((1, D_MODEL)), _const_spec((1, D_MODEL)), _const_spec(wq.shape),
                  _const_spec(wk2.shape), _const_spec(wv.shape), _const_spec((1, D_MODEL)), _const_spec((1, 2 * kvw)),
                  _const_spec(bd.shape), tab, tab, tab],
        out_specs=[row(D_MODEL), row(2 * kvw),
                   pl.BlockSpec((1, kvw, tm), lambda i: (i // (seq // tm), 0, i % (seq // tm))),
                   pl.BlockSpec((tm // MB_BLOCK, 1, 2 * kvw), lambda i: (i, 0, 0))],
        out_shape=[jax.ShapeDtypeStruct((t, D_MODEL), jnp.bfloat16),
                   jax.ShapeDtypeStruct((t, 2 * kvw), jnp.bfloat16),
                   jax.ShapeDtypeStruct((t // seq, kvw, seq), jnp.bfloat16),
                   jax.ShapeDtypeStruct((t // MB_BLOCK, 1, 2 * kvw), jnp.float32)],
        compiler_params=_params(("parallel",)),
        name="proj",
    )(h, gq, gkv, wq, wk2, wv, qn, kn2, bd, cos_t, sin_lo, sin_hi)


MB_GROUPS_PER_STEP = 2


def _moba_kernel(*refs, i):
    q_ref, k_ref, vt_ref, km_ref = refs[:4]
    o_ref, qa_ref, s_ref, acc_ref, vaug_ref = refs[-5:]
    nb = km_ref.shape[1]
    gw = MB_GROUP * MB_HD
    streams = [(gg, h) for gg in range(MB_GROUPS_PER_STEP) for h in range(MB_GROUP)]
    sid = lambda gg, h: gg * MB_GROUP + h
    lanes = lambda gg: slice(gg * LANES, (gg + 1) * LANES)
    nkeys = (i + 1) * MB_BLOCK
    depth = s_ref.shape[1]
    first_half = lax.broadcasted_iota(jnp.int32, (1, LANES), 1) < MB_HD
    zero = jnp.zeros((), jnp.bfloat16)
    for gg in range(MB_GROUPS_PER_STEP):
        for u in range(MB_GROUP // 2):
            t = q_ref[:, gg * gw + u * LANES:gg * gw + (u + 1) * LANES]
            qa_ref[sid(gg, 2 * u)] = jnp.where(first_half, t, zero)
            qa_ref[sid(gg, 2 * u + 1)] = jnp.where(first_half, zero, t)
        vaug_ref[gg, :MB_HD, :] = vt_ref[0, gg * MB_HD:(gg + 1) * MB_HD, :]
        vaug_ref[gg, MB_HD:, :] = jnp.ones((vaug_ref.shape[1] - MB_HD, nkeys), jnp.bfloat16)

    blk = lax.broadcasted_iota(jnp.int32, (nb, MB_BLOCK), 0)
    past = blk < i
    key_i = lax.broadcasted_iota(jnp.int32, (MB_BLOCK, MB_BLOCK), 0)
    qry_i = lax.broadcasted_iota(jnp.int32, (MB_BLOCK, MB_BLOCK), 1)
    causal = key_i <= qry_i

    def scores_of(n, gg):
        maxes = {}
        for h in range(MB_GROUP):
            s = _dot_nt(k_ref[0, n, :, lanes(gg)], qa_ref[sid(gg, h)])
            s_ref[sid(gg, h), n % depth] = s
            maxes[gg, h] = jnp.max(s, axis=0, keepdims=True)
        return maxes

    def weights(s, offset):
        return jnp.exp2(_bf16(s - offset))

    own, chosen = {}, {}
    for gg in range(MB_GROUPS_PER_STEP):
        km = km_ref[0, :, 0, lanes(gg)]
        km_hi = _bf16(km)
        km_lo = _bf16(km - km_hi.astype(jnp.float32))
        own_and_means = jnp.concatenate([k_ref[0, i, :, lanes(gg)], km_hi, km_lo], axis=0)
        for h in range(MB_GROUP):
            r = _dot_nt(own_and_means, qa_ref[sid(gg, h)])
            own[gg, h] = jnp.where(causal, r[:MB_BLOCK], NEG_BIG)
            g = jnp.where(past, r[MB_BLOCK:MB_BLOCK + nb] + r[MB_BLOCK + nb:], -jnp.inf)
            picked = jnp.zeros((nb, MB_BLOCK), jnp.bool_)
            for _ in range(MB_TOPK):
                top = jnp.max(g, axis=0, keepdims=True)
                first = jnp.min(jnp.where(g == top, blk, nb), axis=0, keepdims=True)
                pick = blk == first
                picked = picked | pick
                g = jnp.where(pick, -jnp.inf, g)
            chosen[gg, h] = picked & past
    block_max = {}
    if i > 0:
        for gg in range(MB_GROUPS_PER_STEP):
            block_max.update(scores_of(0, gg))

    ms, ls = {}, {}
    acc_rows = lambda gg, h: slice(sid(gg, h) * MB_HD, (sid(gg, h) + 1) * MB_HD)
    for gg, h in streams:
        mx = jnp.max(own[gg, h], axis=0, keepdims=True)
        pv = _dot(vaug_ref[gg, :, i * MB_BLOCK:nkeys], weights(own[gg, h], mx))
        ms[gg, h] = mx
        ls[gg, h] = pv[MB_HD:MB_HD + 1, :]
        acc_ref[acc_rows(gg, h), :] = pv[:MB_HD, :]

    for n in range(i):
        next_max = {}
        for gg in range(MB_GROUPS_PER_STEP):
            if n + 1 < i:
                next_max.update(scores_of(n + 1, gg))
            for h in range(MB_GROUP):
                use = chosen[gg, h][n:n + 1]
                m_new = jnp.maximum(ms[gg, h], jnp.where(use, block_max[gg, h], NEG_BIG))
                alpha = jnp.exp2(ms[gg, h] - m_new)
                p = weights(s_ref[sid(gg, h), n % depth], jnp.where(use, m_new, -NEG_BIG))
                pv = _dot(vaug_ref[gg, :, n * MB_BLOCK:(n + 1) * MB_BLOCK], p)
                ms[gg, h] = m_new
                ls[gg, h] = alpha * ls[gg, h] + pv[MB_HD:MB_HD + 1, :]
                acc_ref[acc_rows(gg, h), :] = alpha * acc_ref[acc_rows(gg, h), :] + pv[:MB_HD, :]
        block_max = next_max

    for gg, h in streams:
        acc_ref[acc_rows(gg, h), :] = acc_ref[acc_rows(gg, h), :] / ls[gg, h]
    for gg in range(MB_GROUPS_PER_STEP):
        o_ref[:, gg * gw:(gg + 1) * gw] = _bf16(acc_ref[gg * gw:(gg + 1) * gw, :].T)


def _moba(q, k, vt, km, batch, seq):
    nb = seq // MB_BLOCK
    t = batch * seq
    gs = MB_GROUPS_PER_STEP
    gw = gs * MB_GROUP * MB_HD
    n_streams = gs * MB_GROUP
    y = None
    for i in range(nb):
        prev = [] if y is None else [y]
        nkeys = (i + 1) * MB_BLOCK
        y = pl.pallas_call(
            functools.partial(_moba_kernel, i=i),
            grid=(batch, MB_KV_HEADS // gs),
            in_specs=[pl.BlockSpec((MB_BLOCK, gw), lambda b, g, i=i: (b * nb + i, g)),
                      pl.BlockSpec((1, i + 1, MB_BLOCK, gs * LANES), lambda b, g: (b, 0, 0, g)),
                      pl.BlockSpec((1, gs * MB_HD, nkeys), lambda b, g: (b, g, 0)),
                      pl.BlockSpec((1, nb, 1, gs * LANES), lambda b, g: (b, 0, 0, g)),
                      ] + [pl.BlockSpec(memory_space=pl.ANY)] * len(prev),
            out_specs=pl.BlockSpec((MB_BLOCK, gw), lambda b, g, i=i: (b * nb + i, g)),
            out_shape=jax.ShapeDtypeStruct((t, MB_HEADS * MB_HD), jnp.bfloat16),
            input_output_aliases={4: 0} if prev else {},
            scratch_shapes=[pltpu.VMEM((n_streams, MB_BLOCK, LANES), jnp.bfloat16),
                            pltpu.VMEM((n_streams, min(max(i, 1), 3), MB_BLOCK, MB_BLOCK), jnp.float32),
                            pltpu.VMEM((n_streams * MB_HD, MB_BLOCK), jnp.float32),
                            pltpu.VMEM((gs, MB_HD + 16, nkeys), jnp.bfloat16)],
            compiler_params=_params(("parallel", "parallel")),
            name=f"moba{i}",
        )(q, k, vt, km, *prev)
    return y


def _rope_tables(seq):
    inv = ROPE_THETA ** (-np.arange(0, ROPE_DIM, 2, dtype=np.float64) / ROPE_DIM)
    ang = np.arange(seq, dtype=np.float64)[:, None] * inv[None, :]
    cos, sin = np.cos(ang), np.sin(ang)
    zeros = np.zeros((seq, MB_HD - ROPE_DIM))
    zh = np.zeros((seq, ROPE_HALF))
    cos_h = np.concatenate([cos, cos, np.ones_like(zeros)], axis=1)
    lo_h = np.concatenate([-sin, zh, zeros], axis=1)
    hi_h = np.concatenate([zh, sin, zeros], axis=1)
    two = lambda a: jnp.asarray(np.concatenate([a, a], axis=1), jnp.float32)
    return two(cos_h), two(lo_h), two(hi_h)


def kernel(x, p, norm_mix, a_w_in, a_b_gate, a_mh_gain, a_w_out, kv_norm, w_kv, k_norm, b_w_q, b_q_norm, b_w_o,
           norm_ffn, w_gate_up, w_down, norm_ple, w_ple_gate, w_ple_up):
    batch, seq, d = x.shape
    t = batch * seq
    f32 = jnp.float32
    x2 = x.reshape(t, d)
    p2 = p.reshape(p.shape[0], t, PLE_DIM)
    row = lambda a: a.reshape(1, -1).astype(f32)

    w_in = a_w_in[0]
    n_main = 2 * ML_QK_COLS + 2 * ML_V_COLS
    w_main = _bf16(w_in[:, :n_main])
    w_gate = _bf16(jnp.pad(w_in[:, n_main:], ((0, 0), (0, GATE_LANES - 2 * ML_HEADS))))
    b_gate = jnp.pad(a_b_gate[0].astype(f32), (0, GATE_LANES - 2 * ML_HEADS)).reshape(1, GATE_LANES)
    q, k, v, o, gcol = _inproj(x2, row(norm_mix[0]), w_main, w_gate, b_gate)
    grow = jnp.transpose(gcol.reshape(batch, seq, GATE_LANES)[:, :, :3 * ML_HEADS], (0, 2, 1))
    y = _mlstm(q, k, v, o, gcol, grow, row(a_mh_gain[0]), batch, seq)
    h = _post(x2, y, p2, 0, _bf16(a_w_out[0]), row(norm_ffn[0]), _bf16(w_gate_up[0]), _bf16(w_down[0]),
              row(norm_ple[0]), _bf16(w_ple_gate[0]), _bf16(w_ple_up[0]))

    nb = seq // MB_BLOCK
    kvw = MB_KV_HEADS * MB_HD
    cos_t, sin_lo, sin_hi = _rope_tables(seq)
    head = lax.broadcasted_iota(jnp.int32, (kvw, kvw), 0) // MB_HD
    ones_bd = (head == head.T).astype(jnp.bfloat16)
    qn = row(jnp.tile(b_q_norm[0], MB_HEADS)) * MB_Q_SCALE
    kn2 = row(jnp.tile(k_norm, 2 * MB_KV_HEADS))
    w_k = _bf16(w_kv[:, :kvw]).reshape(d, MB_KV_HEADS, 1, MB_HD)
    w_k2 = jnp.broadcast_to(w_k, (d, MB_KV_HEADS, 2, MB_HD)).reshape(d, 2 * kvw)
    q, k2, vt, km2 = _proj(h, row(norm_mix[1]), row(kv_norm), _bf16(b_w_q[0]), w_k2, _bf16(w_kv[:, kvw:]), qn, kn2,
                           ones_bd, cos_t, sin_lo, sin_hi, seq)
    y = _moba(q, k2.reshape(batch, nb, MB_BLOCK, 2 * kvw), vt, km2.reshape(batch, nb, 1, 2 * kvw), batch, seq)
    h = _post(h, y, p2, 1, _bf16(b_w_o[0]), row(norm_ffn[1]), _bf16(w_gate_up[1]), _bf16(w_down[1]),
              row(norm_ple[1]), _bf16(w_ple_gate[1]), _bf16(w_ple_up[1]))
    return h.reshape(batch, seq, d)
```

```python
import functools
import math

import jax
import jax.numpy as jnp
from jax import lax
from jax.experimental import pallas as pl
from jax.experimental.pallas import tpu as pltpu
import numpy as np

D_MODEL = 1024
PLE_DIM = 256
EPS = 1e-6

ML_HEADS = 8
ML_DV = 128
ML_DQK = 64
GATE_CAP = 15.0
ML_QK_COLS = ML_HEADS * ML_DQK
ML_V_COLS = ML_HEADS * ML_DV
ML_CHUNK = 256

MB_HEADS = 16
MB_KV_HEADS = 4
MB_HD = 64
MB_GROUP = MB_HEADS // MB_KV_HEADS
MB_BLOCK = 256
MB_TOPK = 3
ROPE_THETA = 500000.0
ROPE_DIM = MB_HD // 4
ROPE_HALF = ROPE_DIM // 2
MB_Q_SCALE = MB_HD ** -0.5 * math.log2(math.e)

FFN_HIDDEN = 2816

LANES = 128
GATE_LANES = LANES
NEG_BIG = -1e30
VMEM_LIMIT = 56 * 1024 * 1024

ROW_TILE = 512
POST_ROW_TILE = 512
FFN_HIDDEN_SPLITS = ((0, 1536), (1536, FFN_HIDDEN))


def _bf16(x):
    return x.astype(jnp.bfloat16)


def _dot(a, b):
    return jnp.dot(a, b, preferred_element_type=jnp.float32)


def _dot_nt(a, b):
    return lax.dot_general(a, b, (((1,), (1,)), ((), ())), preferred_element_type=jnp.float32)


def _dot_tn(a, b):
    return lax.dot_general(a, b, (((0,), (0,)), ((), ())), preferred_element_type=jnp.float32)


def _rms_rows(x):
    return x * lax.rsqrt(jnp.mean(x * x, axis=-1, keepdims=True) + EPS)


def _const_spec(shape):
    nd = len(shape)
    return pl.BlockSpec(shape, lambda *_: (0,) * nd, pipeline_mode=pl.Buffered(1))


def _params(sem):
    return pltpu.CompilerParams(dimension_semantics=sem, vmem_limit_bytes=VMEM_LIMIT)


def _inproj_kernel(x_ref, g_ref, w_ref, wg_ref, bg_ref, q_ref, k_ref, v_ref, o_ref, gate_ref, gate_t_ref):
    hn = _bf16(_rms_rows(x_ref[...]) * g_ref[...])
    c0, c1, c2, c3 = ML_QK_COLS, 2 * ML_QK_COLS, 2 * ML_QK_COLS + ML_V_COLS, 2 * ML_QK_COLS + 2 * ML_V_COLS
    pre = _dot(hn, wg_ref[...]) + bg_ref[...]
    cap = GATE_CAP * jnp.tanh(pre * (1.0 / GATE_CAP))
    logf = jnp.minimum(cap, 0.0) - jnp.log1p(jnp.exp(-jnp.abs(cap)))
    rows = cap.shape[0]
    row_in_chunk = lax.broadcasted_iota(jnp.int32, cap.shape, 0) % ML_CHUNK
    cum = logf
    d = 1
    while d < ML_CHUNK:
        cum = cum + jnp.where(row_in_chunk >= d, pltpu.roll(cum, d, 0), 0.0)
        d *= 2
    li_shift = pltpu.roll(cap, ML_HEADS, 1)
    parts = []
    for c in range(rows // ML_CHUNK):
        sl = slice(c * ML_CHUNK, (c + 1) * ML_CHUNK)
        g_last = cum[(c + 1) * ML_CHUNK - 1:(c + 1) * ML_CHUNK, :]
        parts.append(g_last - cum[sl] + li_shift[sl])
    wlog = pltpu.roll(jnp.concatenate(parts, axis=0), ML_HEADS, 1)
    lane = lax.broadcasted_iota(jnp.int32, cap.shape, 1)
    gates = jnp.where(lane < ML_HEADS, cap, jnp.where(lane < 2 * ML_HEADS, cum, wlog))
    gate_ref[...] = gates
    gate_t_ref[0] = gates.T

    q_ref[...] = _bf16(_dot(hn, w_ref[:, 0:c0]) * (ML_DQK ** -0.5))
    k_ref[...] = _bf16(_dot(hn, w_ref[:, c0:c1]))
    v_ref[...] = _bf16(_dot(hn, w_ref[:, c1:c2]))
    o_ref[...] = _bf16(_dot(hn, w_ref[:, c2:c3]))


def _inproj(x2, g, w, wg, bg, seq):
    t = x2.shape[0]
    tm = ROW_TILE
    row = lambda n: pl.BlockSpec((tm, n), lambda i: (i, 0))
    return pl.pallas_call(
        _inproj_kernel,
        grid=(t // tm,),
        in_specs=[row(D_MODEL), _const_spec((1, D_MODEL)), _const_spec(w.shape), _const_spec(wg.shape),
                  _const_spec((1, GATE_LANES))],
        out_specs=[row(ML_QK_COLS), row(ML_QK_COLS), row(ML_V_COLS), row(ML_V_COLS), row(GATE_LANES),
                   pl.BlockSpec((1, GATE_LANES, tm), lambda i: (i // (seq // tm), 0, i % (seq // tm)))],
        out_shape=[jax.ShapeDtypeStruct((t, ML_QK_COLS), jnp.bfloat16),
                   jax.ShapeDtypeStruct((t, ML_QK_COLS), jnp.bfloat16),
                   jax.ShapeDtypeStruct((t, ML_V_COLS), jnp.bfloat16),
                   jax.ShapeDtypeStruct((t, ML_V_COLS), jnp.bfloat16),
                   jax.ShapeDtypeStruct((t, GATE_LANES), jnp.float32),
                   jax.ShapeDtypeStruct((t // seq, GATE_LANES, seq), jnp.float32)],
        compiler_params=_params(("parallel",)),
        name="inproj",
    )(x2, g, w, wg, bg)


def _mlstm_kernel(q_ref, k_ref, v_ref, o_ref, gcol_ref, grow_ref, gain_ref, y_ref, c_ref):
    L = ML_CHUNK

    @pl.when(pl.program_id(1) == 0)
    def _():
        c_ref[...] = jnp.zeros_like(c_ref)

    lane_row = lax.broadcasted_iota(jnp.int32, (1, LANES), 1)
    r_i = lax.broadcasted_iota(jnp.int32, (L, L), 0)
    c_i = lax.broadcasted_iota(jnp.int32, (L, L), 1)
    causal = r_i >= c_i
    gc = gcol_ref[...]
    egc = jnp.exp(gc)
    gain = gain_ref[...]
    heads = range(ML_HEADS)
    pair = lambda h: slice((h // 2) * LANES, (h // 2 + 1) * LANES)
    cols = lambda h: slice(h * ML_DV, (h + 1) * ML_DV)
    col = lambda a, c: a[:, c:c + 1]

    ones = jnp.ones((L, ML_DV), jnp.bfloat16)
    vs = [jnp.concatenate([v_ref[:, cols(h)], ones], axis=1) for h in heads]

    scores, inter = [], []
    for h in heads:
        head_lanes = (lane_row // ML_DQK) == (h % 2)
        q = jnp.where(head_lanes, q_ref[:, pair(h)], jnp.zeros((), jnp.bfloat16))
        scores.append(_dot_nt(q, k_ref[:, pair(h)]))
        inter.append(_dot(q, _bf16(c_ref[h])))

    ps = []
    for h in heads:
        c_row = grow_ref[0, h:h + 1, :] - grow_ref[0, ML_HEADS + h:ML_HEADS + h + 1, :]
        dmat = jnp.where(causal, col(gc, ML_HEADS + h) + c_row, NEG_BIG)
        ps.append(_bf16(scores[h] * jnp.exp(dmat)))

    for h in heads:
        both = _dot(ps[h], vs[h]) + inter[h] * col(egc, ML_HEADS + h)
        hc = both[:, :ML_DV] / jnp.maximum(jnp.abs(both[:, ML_DV:]), 1.0)
        hn = _rms_rows(hc) * gain
        y_ref[:, cols(h)] = _bf16(jax.nn.sigmoid(o_ref[:, cols(h)].astype(jnp.float32)) * hn)

    for h in heads:
        decay = jnp.exp(grow_ref[0, ML_HEADS + h:ML_HEADS + h + 1, L - 1:L])
        kw = k_ref[:, pair(h)].astype(jnp.float32) * col(egc, 2 * ML_HEADS + h)
        c_ref[h] = decay * c_ref[h] + _dot_tn(_bf16(kw), vs[h])


def _mlstm(q, k, v, o, gcol, grow, gain, batch, seq):
    L = ML_CHUNK
    nc = seq // L
    t = batch * seq
    row = lambda n: pl.BlockSpec((L, n), lambda b, c: (b * nc + c, 0))
    return pl.pallas_call(
        _mlstm_kernel,
        grid=(batch, nc),
        in_specs=[row(ML_QK_COLS), row(ML_QK_COLS), row(ML_V_COLS), row(ML_V_COLS), row(GATE_LANES),
                  pl.BlockSpec((1, 3 * ML_HEADS, L), lambda b, c: (b, 0, c)),
                  pl.BlockSpec((1, ML_DV), lambda b, c: (0, 0))],
        out_specs=row(ML_V_COLS),
        out_shape=jax.ShapeDtypeStruct((t, ML_V_COLS), jnp.bfloat16),
        scratch_shapes=[pltpu.VMEM((ML_HEADS, LANES, 2 * ML_DV), jnp.float32)],
        compiler_params=_params(("parallel", "arbitrary")),
        name="mlstm",
    )(q, k, v, o, gcol, grow, gain)


def _post_kernel(h_ref, y_ref, p_ref, wo_ref, gf_ref, wgu_ref, wd_ref, gp_ref, wpg_ref, wpu_ref, out_ref):
    h1 = h_ref[...] + _dot(y_ref[...], wo_ref[...])
    hn = _bf16(_rms_rows(h1) * gf_ref[...])
    h2 = h1
    for lo, hi in FFN_HIDDEN_SPLITS:
        gate_pre = _dot(hn, wgu_ref[:, lo:hi])
        up_pre = _dot(hn, wgu_ref[:, FFN_HIDDEN + lo:FFN_HIDDEN + hi])
        h2 = h2 + _dot(_bf16(jax.nn.silu(gate_pre) * up_pre), wd_ref[lo:hi, :])
    hp = _bf16(_rms_rows(h2) * gp_ref[...])
    gate = jax.nn.sigmoid(_dot(hp, wpg_ref[...]))
    up = _dot(_bf16(p_ref[...]), wpu_ref[...])
    out_ref[...] = h2 + up * gate


def _post(h, y, p, layer, wo, gf, wgu, wd, gp, wpg, wpu):
    t = h.shape[0]
    tm = POST_ROW_TILE
    row = lambda n: pl.BlockSpec((tm, n), lambda i: (i, 0))
    p_spec = pl.BlockSpec((None, tm, PLE_DIM), lambda i: (layer, i, 0))
    return pl.pallas_call(
        _post_kernel,
        grid=(t // tm,),
        in_specs=[row(D_MODEL), row(D_MODEL), p_spec, _const_spec(wo.shape), _const_spec((1, D_MODEL)),
                  _const_spec(wgu.shape), _const_spec(wd.shape), _const_spec((1, D_MODEL)),
                  _const_spec(wpg.shape), _const_spec(wpu.shape)],
        out_specs=row(D_MODEL),
        out_shape=jax.ShapeDtypeStruct((t, D_MODEL), jnp.float32),
        compiler_params=_params(("parallel",)),
        name="post",
    )(h, y, p, wo, gf, wgu, wd, gp, wpg, wpu)


def _head_mean_square(x, ones_bd):
    return _dot(_bf16(x * x), ones_bd) * (1.0 / MB_HD)


def _norm_rope(x, ms, gain, cos_t, sin_lo, sin_hi):
    xn = x * lax.rsqrt(ms + EPS) * gain
    outs = []
    for s in range(x.shape[1] // LANES):
        xs = xn[:, s * LANES:(s + 1) * LANES]
        outs.append(xs * cos_t + pltpu.roll(xs, LANES - ROPE_HALF, 1) * sin_lo + pltpu.roll(xs, ROPE_HALF, 1) * sin_hi)
    return jnp.concatenate(outs, axis=1)


def _proj_kernel(h_ref, gq_ref, gkv_ref, wq_ref, wk_ref, wv_ref, qn_ref, kn_ref, bd_ref, cos_ref, slo_ref, shi_ref,
                 q_ref, k_ref, vt_ref, km_ref):
    r = _rms_rows(h_ref[...])
    hq = _bf16(r * gq_ref[...])
    hkv = _bf16(r * gkv_ref[...])
    tables = (cos_ref[...], slo_ref[...], shi_ref[...])
    bd = bd_ref[...]
    kvw = MB_KV_HEADS * MB_HD
    slab = lambda s: slice(s * kvw, (s + 1) * kvw)
    nq = D_MODEL // kvw
    raw = [_dot(hq, wq_ref[:, slab(s)]) for s in range(nq)] + [_dot(hkv, wk_ref[:, slab(s)]) for s in range(2)]
    gains = [qn_ref[:, slab(s)] for s in range(nq)] + [kn_ref[:, slab(s)] for s in range(2)]
    vs = _dot(hkv, wv_ref[...])
    means = [_head_mean_square(x, bd) for x in raw]
    vt_ref[0] = _bf16(vs.T)
    for s in range(nq):
        q_ref[:, slab(s)] = _bf16(_norm_rope(raw[s], means[s], gains[s], *tables))
    for s in range(2):
        ks = _norm_rope(raw[nq + s], means[nq + s], gains[nq + s], *tables)
        k_ref[:, slab(s)] = _bf16(ks)
        for b in range(ks.shape[0] // MB_BLOCK):
            km_ref[b, :, slab(s)] = jnp.mean(ks[b * MB_BLOCK:(b + 1) * MB_BLOCK], axis=0, keepdims=True)


def _proj(h, gq, gkv, wq, wk2, wv, qn, kn2, bd, cos_t, sin_lo, sin_hi, seq):
    t = h.shape[0]
    tm = ROW_TILE
    kvw = MB_KV_HEADS * MB_HD
    row = lambda n: pl.BlockSpec((tm, n), lambda i: (i, 0))
    tab = pl.BlockSpec((tm, LANES), lambda i: (i % (seq // tm), 0))
    return pl.pallas_call(
        _proj_kernel,
        grid=(t // tm,),
        in_specs=[row(D_MODEL), _const_spec((1, D_MODEL)), _const_spec((1, D_MODEL)), _const_spec(wq.shape),
                  _const_spec(wk2.shape), _const_spec(wv.shape), _const_spec((1, D_MODEL)), _const_spec((1, 2 * kvw)),
                  _const_spec(bd.shape), tab, tab, tab],
        out_specs=[row(D_MODEL), row(2 * kvw),
                   pl.BlockSpec((1, kvw, tm), lambda i: (i // (seq // tm), 0, i % (seq // tm))),
                   pl.BlockSpec((tm // MB_BLOCK, 1, 2 * kvw), lambda i: (i, 0, 0))],
        out_shape=[jax.ShapeDtypeStruct((t, D_MODEL), jnp.bfloat16),
                   jax.ShapeDtypeStruct((t, 2 * kvw), jnp.bfloat16),
                   jax.ShapeDtypeStruct((t // seq, kvw, seq), jnp.bfloat16),
                   jax.ShapeDtypeStruct((t // MB_BLOCK, 1, 2 * kvw), jnp.float32)],
        compiler_params=_params(("parallel",)),
        name="proj",
    )(h, gq, gkv, wq, wk2, wv, qn, kn2, bd, cos_t, sin_lo, sin_hi)


MB_GROUPS_PER_STEP = 2


def _moba_kernel(*refs, i):
    q_ref, k_ref, vt_ref, km_ref = refs[:4]
    o_ref, qa_ref, s_ref, acc_ref, vaug_ref = refs[-5:]
    nb = km_ref.shape[1]
    gw = MB_GROUP * MB_HD
    streams = [(gg, h) for gg in range(MB_GROUPS_PER_STEP) for h in range(MB_GROUP)]
    sid = lambda gg, h: gg * MB_GROUP + h
    lanes = lambda gg: slice(gg * LANES, (gg + 1) * LANES)
    nkeys = (i + 1) * MB_BLOCK
    depth = s_ref.shape[1]
    first_half = lax.broadcasted_iota(jnp.int32, (1, LANES), 1) < MB_HD
    zero = jnp.zeros((), jnp.bfloat16)
    for gg in range(MB_GROUPS_PER_STEP):
        for u in range(MB_GROUP // 2):
            t = q_ref[:, gg * gw + u * LANES:gg * gw + (u + 1) * LANES]
            qa_ref[sid(gg, 2 * u)] = jnp.where(first_half, t, zero)
            qa_ref[sid(gg, 2 * u + 1)] = jnp.where(first_half, zero, t)
        vaug_ref[gg, :MB_HD, :] = vt_ref[0, gg * MB_HD:(gg + 1) * MB_HD, :]
        vaug_ref[gg, MB_HD:, :] = jnp.ones((vaug_ref.shape[1] - MB_HD, nkeys), jnp.bfloat16)

    blk = lax.broadcasted_iota(jnp.int32, (nb, MB_BLOCK), 0)
    past = blk < i
    key_i = lax.broadcasted_iota(jnp.int32, (MB_BLOCK, MB_BLOCK), 0)
    qry_i = lax.broadcasted_iota(jnp.int32, (MB_BLOCK, MB_BLOCK), 1)
    causal = key_i <= qry_i

    def scores_of(n, gg):
        maxes = {}
        for h in range(MB_GROUP):
            s = _dot_nt(k_ref[0, n, :, lanes(gg)], qa_ref[sid(gg, h)])
            s_ref[sid(gg, h), n % depth] = s
            maxes[gg, h] = jnp.max(s, axis=0, keepdims=True)
        return maxes

    def weights(s, offset):
        return jnp.exp2(_bf16(s - offset))

    own, chosen = {}, {}
    for gg in range(MB_GROUPS_PER_STEP):
        km = km_ref[0, :, 0, lanes(gg)]
        km_hi = _bf16(km)
        km_lo = _bf16(km - km_hi.astype(jnp.float32))
        own_and_means = jnp.concatenate([k_ref[0, i, :, lanes(gg)], km_hi, km_lo], axis=0)
        for h in range(MB_GROUP):
            r = _dot_nt(own_and_means, qa_ref[sid(gg, h)])
            own[gg, h] = jnp.where(causal, r[:MB_BLOCK], NEG_BIG)
            g = jnp.where(past, r[MB_BLOCK:MB_BLOCK + nb] + r[MB_BLOCK + nb:], -jnp.inf)
            picked = jnp.zeros((nb, MB_BLOCK), jnp.bool_)
            for _ in range(MB_TOPK):
                top = jnp.max(g, axis=0, keepdims=True)
                first = jnp.min(jnp.where(g == top, blk, nb), axis=0, keepdims=True)
                pick = blk == first
                picked = picked | pick
                g = jnp.where(pick, -jnp.inf, g)
            chosen[gg, h] = picked & past
    block_max = {}
    if i > 0:
        for gg in range(MB_GROUPS_PER_STEP):
            block_max.update(scores_of(0, gg))

    ms, ls = {}, {}
    acc_rows = lambda gg, h: slice(sid(gg, h) * MB_HD, (sid(gg, h) + 1) * MB_HD)
    for gg, h in streams:
        mx = jnp.max(own[gg, h], axis=0, keepdims=True)
        pv = _dot(vaug_ref[gg, :, i * MB_BLOCK:nkeys], weights(own[gg, h], mx))
        ms[gg, h] = mx
        ls[gg, h] = pv[MB_HD:MB_HD + 1, :]
        acc_ref[acc_rows(gg, h), :] = pv[:MB_HD, :]

    for n in range(i):
        next_max = {}
        for gg in range(MB_GROUPS_PER_STEP):
            if n + 1 < i:
                next_max.update(scores_of(n + 1, gg))
            for h in range(MB_GROUP):
                use = chosen[gg, h][n:n + 1]
                m_new = jnp.maximum(ms[gg, h], jnp.where(use, block_max[gg, h], NEG_BIG))
                alpha = jnp.exp2(ms[gg, h] - m_new)
                p = weights(s_ref[sid(gg, h), n % depth], jnp.where(use, m_new, -NEG_BIG))
                pv = _dot(vaug_ref[gg, :, n * MB_BLOCK:(n + 1) * MB_BLOCK], p)
                ms[gg, h] = m_new
                ls[gg, h] = alpha * ls[gg, h] + pv[MB_HD:MB_HD + 1, :]
                acc_ref[acc_rows(gg, h), :] = alpha * acc_ref[acc_rows(gg, h), :] + pv[:MB_HD, :]
        block_max = next_max

    for gg, h in streams:
        acc_ref[acc_rows(gg, h), :] = acc_ref[acc_rows(gg, h), :] / ls[gg, h]
    for gg in range(MB_GROUPS_PER_STEP):
        o_ref[:, gg * gw:(gg + 1) * gw] = _bf16(acc_ref[gg * gw:(gg + 1) * gw, :].T)


def _moba(q, k, vt, km, batch, seq):
    nb = seq // MB_BLOCK
    t = batch * seq
    gs = MB_GROUPS_PER_STEP
    gw = gs * MB_GROUP * MB_HD
    n_streams = gs * MB_GROUP
    y = jnp.zeros((t, MB_HEADS * MB_HD), jnp.bfloat16)
    for i in range(nb):
        nkeys = (i + 1) * MB_BLOCK
        y = pl.pallas_call(
            functools.partial(_moba_kernel, i=i),
            grid=(batch, MB_KV_HEADS // gs),
            in_specs=[pl.BlockSpec((MB_BLOCK, gw), lambda b, g, i=i: (b * nb + i, g)),
                      pl.BlockSpec((1, i + 1, MB_BLOCK, gs * LANES), lambda b, g: (b, 0, 0, g)),
                      pl.BlockSpec((1, gs * MB_HD, nkeys), lambda b, g: (b, g, 0)),
                      pl.BlockSpec((1, nb, 1, gs * LANES), lambda b, g: (b, 0, 0, g)),
                      pl.BlockSpec(memory_space=pl.ANY)],
            out_specs=pl.BlockSpec((MB_BLOCK, gw), lambda b, g, i=i: (b * nb + i, g)),
            out_shape=jax.ShapeDtypeStruct((t, MB_HEADS * MB_HD), jnp.bfloat16),
            input_output_aliases={4: 0},
            scratch_shapes=[pltpu.VMEM((n_streams, MB_BLOCK, LANES), jnp.bfloat16),
                            pltpu.VMEM((n_streams, min(max(i, 1), 3), MB_BLOCK, MB_BLOCK), jnp.float32),
                            pltpu.VMEM((n_streams * MB_HD, MB_BLOCK), jnp.float32),
                            pltpu.VMEM((gs, MB_HD + 16, nkeys), jnp.bfloat16)],
            compiler_params=_params(("parallel", "parallel")),
            name=f"moba{i}",
        )(q, k, vt, km, y)
    return y


def _rope_tables(seq):
    inv = ROPE_THETA ** (-np.arange(0, ROPE_DIM, 2, dtype=np.float64) / ROPE_DIM)
    ang = np.arange(seq, dtype=np.float64)[:, None] * inv[None, :]
    cos, sin = np.cos(ang), np.sin(ang)
    zeros = np.zeros((seq, MB_HD - ROPE_DIM))
    zh = np.zeros((seq, ROPE_HALF))
    cos_h = np.concatenate([cos, cos, np.ones_like(zeros)], axis=1)
    lo_h = np.concatenate([-sin, zh, zeros], axis=1)
    hi_h = np.concatenate([zh, sin, zeros], axis=1)
    two = lambda a: jnp.asarray(np.concatenate([a, a], axis=1), jnp.float32)
    return two(cos_h), two(lo_h), two(hi_h)


def kernel(x, p, norm_mix, a_w_in, a_b_gate, a_mh_gain, a_w_out, kv_norm, w_kv, k_norm, b_w_q, b_q_norm, b_w_o,
           norm_ffn, w_gate_up, w_down, norm_ple, w_ple_gate, w_ple_up):
    batch, seq, d = x.shape
    t = batch * seq
    f32 = jnp.float32
    x2 = x.reshape(t, d)
    p2 = p.reshape(p.shape[0], t, PLE_DIM)
    row = lambda a: a.reshape(1, -1).astype(f32)

    w_in = a_w_in[0]
    n_main = 2 * ML_QK_COLS + 2 * ML_V_COLS
    w_main = _bf16(w_in[:, :n_main])
    w_gate = _bf16(jnp.pad(w_in[:, n_main:], ((0, 0), (0, GATE_LANES - 2 * ML_HEADS))))
    b_gate = jnp.pad(a_b_gate[0].astype(f32), (0, GATE_LANES - 2 * ML_HEADS)).reshape(1, GATE_LANES)
    q, k, v, o, gcol, grow = _inproj(x2, row(norm_mix[0]), w_main, w_gate, b_gate, seq)
    y = _mlstm(q, k, v, o, gcol, grow, row(a_mh_gain[0]), batch, seq)
    h = _post(x2, y, p2, 0, _bf16(a_w_out[0]), row(norm_ffn[0]), _bf16(w_gate_up[0]), _bf16(w_down[0]),
              row(norm_ple[0]), _bf16(w_ple_gate[0]), _bf16(w_ple_up[0]))

    nb = seq // MB_BLOCK
    kvw = MB_KV_HEADS * MB_HD
    cos_t, sin_lo, sin_hi = _rope_tables(seq)
    head = lax.broadcasted_iota(jnp.int32, (kvw, kvw), 0) // MB_HD
    ones_bd = (head == head.T).astype(jnp.bfloat16)
    qn = row(jnp.tile(b_q_norm[0], MB_HEADS)) * MB_Q_SCALE
    kn2 = row(jnp.tile(k_norm, 2 * MB_KV_HEADS))
    w_k = _bf16(w_kv[:, :kvw]).reshape(d, MB_KV_HEADS, 1, MB_HD)
    w_k2 = jnp.broadcast_to(w_k, (d, MB_KV_HEADS, 2, MB_HD)).reshape(d, 2 * kvw)
    q, k2, vt, km2 = _proj(h, row(norm_mix[1]), row(kv_norm), _bf16(b_w_q[0]), w_k2, _bf16(w_kv[:, kvw:]), qn, kn2,
                           ones_bd, cos_t, sin_lo, sin_hi, seq)
    y = _moba(q, k2.reshape(batch, nb, MB_BLOCK, 2 * kvw), vt, km2.reshape(batch, nb, 1, 2 * kvw), batch, seq)
    h = _post(h, y, p2, 1, _bf16(b_w_o[0]), row(norm_ffn[1]), _bf16(w_gate_up[1]), _bf16(w_down[1]),
              row(norm_ple[1]), _bf16(w_ple_gate[1]), _bf16(w_ple_up[1]))
    return h.reshape(batch, seq, d)
```

```python
import functools
import math

import jax
import jax.numpy as jnp
from jax import lax
from jax.experimental import pallas as pl
from jax.experimental.pallas import tpu as pltpu
import numpy as np

D_MODEL = 1024
PLE_DIM = 256
EPS = 1e-6

ML_HEADS = 8
ML_DV = 128
ML_DQK = 64
GATE_CAP = 15.0
ML_QK_COLS = ML_HEADS * ML_DQK
ML_V_COLS = ML_HEADS * ML_DV
ML_CHUNK = 256

MB_HEADS = 16
MB_KV_HEADS = 4
MB_HD = 64
MB_GROUP = MB_HEADS // MB_KV_HEADS
MB_BLOCK = 256
MB_TOPK = 3
ROPE_THETA = 500000.0
ROPE_DIM = MB_HD // 4
ROPE_HALF = ROPE_DIM // 2
MB_Q_SCALE = MB_HD ** -0.5 * math.log2(math.e)

FFN_HIDDEN = 2816

LANES = 128
GATE_LANES = LANES
NEG_BIG = -1e30
VMEM_LIMIT = 56 * 1024 * 1024

ROW_TILE = 512
POST_ROW_TILE = 512
FFN_HIDDEN_SPLITS = ((0, 1536), (1536, FFN_HIDDEN))


def _bf16(x):
    return x.astype(jnp.bfloat16)


def _dot(a, b):
    return jnp.dot(a, b, preferred_element_type=jnp.float32)


def _dot_nt(a, b):
    return lax.dot_general(a, b, (((1,), (1,)), ((), ())), preferred_element_type=jnp.float32)


def _dot_tn(a, b):
    return lax.dot_general(a, b, (((0,), (0,)), ((), ())), preferred_element_type=jnp.float32)


def _rms_rows(x):
    return x * lax.rsqrt(jnp.mean(x * x, axis=-1, keepdims=True) + EPS)


def _const_spec(shape):
    nd = len(shape)
    return pl.BlockSpec(shape, lambda *_: (0,) * nd, pipeline_mode=pl.Buffered(1))


def _params(sem):
    return pltpu.CompilerParams(dimension_semantics=sem, vmem_limit_bytes=VMEM_LIMIT)


def _inproj_kernel(x_ref, g_ref, w_ref, wg_ref, bg_ref, q_ref, k_ref, v_ref, o_ref, gate_ref, gate_t_ref):
    hn = _bf16(_rms_rows(x_ref[...]) * g_ref[...])
    c0, c1, c2, c3 = ML_QK_COLS, 2 * ML_QK_COLS, 2 * ML_QK_COLS + ML_V_COLS, 2 * ML_QK_COLS + 2 * ML_V_COLS
    pre = _dot(hn, wg_ref[...]) + bg_ref[...]
    cap = GATE_CAP * jnp.tanh(pre * (1.0 / GATE_CAP))
    logf = jnp.minimum(cap, 0.0) - jnp.log1p(jnp.exp(-jnp.abs(cap)))
    rows = cap.shape[0]
    row_in_chunk = lax.broadcasted_iota(jnp.int32, cap.shape, 0) % ML_CHUNK
    cum = logf
    d = 1
    while d < ML_CHUNK:
        cum = cum + jnp.where(row_in_chunk >= d, pltpu.roll(cum, d, 0), 0.0)
        d *= 2
    li_shift = pltpu.roll(cap, ML_HEADS, 1)
    parts = []
    for c in range(rows // ML_CHUNK):
        sl = slice(c * ML_CHUNK, (c + 1) * ML_CHUNK)
        g_last = cum[(c + 1) * ML_CHUNK - 1:(c + 1) * ML_CHUNK, :]
        parts.append(g_last - cum[sl] + li_shift[sl])
    wlog = pltpu.roll(jnp.concatenate(parts, axis=0), ML_HEADS, 1)
    lane = lax.broadcasted_iota(jnp.int32, cap.shape, 1)
    gates = jnp.where(lane < ML_HEADS, cap, jnp.where(lane < 2 * ML_HEADS, cum, wlog))
    gate_ref[...] = gates
    gate_t_ref[0] = gates.T

    q_ref[...] = _bf16(_dot(hn, w_ref[:, 0:c0]) * (ML_DQK ** -0.5))
    k_ref[...] = _bf16(_dot(hn, w_ref[:, c0:c1]))
    v_ref[...] = _bf16(_dot(hn, w_ref[:, c1:c2]))
    o_ref[...] = _bf16(_dot(hn, w_ref[:, c2:c3]))


def _inproj(x2, g, w, wg, bg, seq):
    t = x2.shape[0]
    tm = ROW_TILE
    row = lambda n: pl.BlockSpec((tm, n), lambda i: (i, 0))
    return pl.pallas_call(
        _inproj_kernel,
        grid=(t // tm,),
        in_specs=[row(D_MODEL), _const_spec((1, D_MODEL)), _const_spec(w.shape), _const_spec(wg.shape),
                  _const_spec((1, GATE_LANES))],
        out_specs=[row(ML_QK_COLS), row(ML_QK_COLS), row(ML_V_COLS), row(ML_V_COLS), row(GATE_LANES),
                   pl.BlockSpec((1, GATE_LANES, tm), lambda i: (i // (seq // tm), 0, i % (seq // tm)))],
        out_shape=[jax.ShapeDtypeStruct((t, ML_QK_COLS), jnp.bfloat16),
                   jax.ShapeDtypeStruct((t, ML_QK_COLS), jnp.bfloat16),
                   jax.ShapeDtypeStruct((t, ML_V_COLS), jnp.bfloat16),
                   jax.ShapeDtypeStruct((t, ML_V_COLS), jnp.bfloat16),
                   jax.ShapeDtypeStruct((t, GATE_LANES), jnp.float32),
                   jax.ShapeDtypeStruct((t // seq, GATE_LANES, seq), jnp.float32)],
        compiler_params=_params(("parallel",)),
        name="inproj",
    )(x2, g, w, wg, bg)


def _mlstm_kernel(q_ref, k_ref, v_ref, o_ref, gcol_ref, grow_ref, gain_ref, y_ref, c_ref):
    L = ML_CHUNK

    @pl.when(pl.program_id(1) == 0)
    def _():
        c_ref[...] = jnp.zeros_like(c_ref)

    lane_row = lax.broadcasted_iota(jnp.int32, (1, LANES), 1)
    r_i = lax.broadcasted_iota(jnp.int32, (L, L), 0)
    c_i = lax.broadcasted_iota(jnp.int32, (L, L), 1)
    causal = r_i >= c_i
    gc = gcol_ref[...]
    egc = jnp.exp(gc)
    gain = gain_ref[...]
    heads = range(ML_HEADS)
    pair = lambda h: slice((h // 2) * LANES, (h // 2 + 1) * LANES)
    cols = lambda h: slice(h * ML_DV, (h + 1) * ML_DV)
    col = lambda a, c: a[:, c:c + 1]

    ones = jnp.ones((L, ML_DV), jnp.bfloat16)
    vs = [jnp.concatenate([v_ref[:, cols(h)], ones], axis=1) for h in heads]

    scores, inter = [], []
    for h in heads:
        head_lanes = (lane_row // ML_DQK) == (h % 2)
        q = jnp.where(head_lanes, q_ref[:, pair(h)], jnp.zeros((), jnp.bfloat16))
        scores.append(_dot_nt(q, k_ref[:, pair(h)]))
        inter.append(_dot(q, _bf16(c_ref[h])))

    ps = []
    for h in heads:
        c_row = grow_ref[0, h:h + 1, :] - grow_ref[0, ML_HEADS + h:ML_HEADS + h + 1, :]
        dmat = jnp.where(causal, col(gc, ML_HEADS + h) + c_row, NEG_BIG)
        ps.append(_bf16(scores[h] * jnp.exp(dmat)))

    for h in heads:
        both = _dot(ps[h], vs[h]) + inter[h] * col(egc, ML_HEADS + h)
        hc = both[:, :ML_DV] / jnp.maximum(jnp.abs(both[:, ML_DV:]), 1.0)
        hn = _rms_rows(hc) * gain
        y_ref[:, cols(h)] = _bf16(jax.nn.sigmoid(o_ref[:, cols(h)].astype(jnp.float32)) * hn)

    for h in heads:
        decay = jnp.exp(grow_ref[0, ML_HEADS + h:ML_HEADS + h + 1, L - 1:L])
        kw = k_ref[:, pair(h)].astype(jnp.float32) * col(egc, 2 * ML_HEADS + h)
        c_ref[h] = decay * c_ref[h] + _dot_tn(_bf16(kw), vs[h])


def _mlstm(q, k, v, o, gcol, grow, gain, batch, seq):
    L = ML_CHUNK
    nc = seq // L
    t = batch * seq
    row = lambda n: pl.BlockSpec((L, n), lambda b, c: (b * nc + c, 0))
    return pl.pallas_call(
        _mlstm_kernel,
        grid=(batch, nc),
        in_specs=[row(ML_QK_COLS), row(ML_QK_COLS), row(ML_V_COLS), row(ML_V_COLS), row(GATE_LANES),
                  pl.BlockSpec((1, 3 * ML_HEADS, L), lambda b, c: (b, 0, c)),
                  pl.BlockSpec((1, ML_DV), lambda b, c: (0, 0))],
        out_specs=row(ML_V_COLS),
        out_shape=jax.ShapeDtypeStruct((t, ML_V_COLS), jnp.bfloat16),
        scratch_shapes=[pltpu.VMEM((ML_HEADS, LANES, 2 * ML_DV), jnp.float32)],
        compiler_params=_params(("parallel", "arbitrary")),
        name="mlstm",
    )(q, k, v, o, gcol, grow, gain)


def _post_kernel(h_ref, y_ref, p_ref, wo_ref, gf_ref, wgu_ref, wd_ref, gp_ref, wpg_ref, wpu_ref, out_ref):
    h1 = h_ref[...] + _dot(y_ref[...], wo_ref[...])
    hn = _bf16(_rms_rows(h1) * gf_ref[...])
    h2 = h1
    for lo, hi in FFN_HIDDEN_SPLITS:
        gate_pre = _dot(hn, wgu_ref[:, lo:hi])
        up_pre = _dot(hn, wgu_ref[:, FFN_HIDDEN + lo:FFN_HIDDEN + hi])
        h2 = h2 + _dot(_bf16(jax.nn.silu(gate_pre) * up_pre), wd_ref[lo:hi, :])
    hp = _bf16(_rms_rows(h2) * gp_ref[...])
    gate = jax.nn.sigmoid(_dot(hp, wpg_ref[...]))
    up = _dot(_bf16(p_ref[...]), wpu_ref[...])
    out_ref[...] = h2 + up * gate


def _post(h, y, p, layer, wo, gf, wgu, wd, gp, wpg, wpu):
    t = h.shape[0]
    tm = POST_ROW_TILE
    row = lambda n: pl.BlockSpec((tm, n), lambda i: (i, 0))
    p_spec = pl.BlockSpec((None, tm, PLE_DIM), lambda i: (layer, i, 0))
    return pl.pallas_call(
        _post_kernel,
        grid=(t // tm,),
        in_specs=[row(D_MODEL), row(D_MODEL), p_spec, _const_spec(wo.shape), _const_spec((1, D_MODEL)),
                  _const_spec(wgu.shape), _const_spec(wd.shape), _const_spec((1, D_MODEL)),
                  _const_spec(wpg.shape), _const_spec(wpu.shape)],
        out_specs=row(D_MODEL),
        out_shape=jax.ShapeDtypeStruct((t, D_MODEL), jnp.float32),
        compiler_params=_params(("parallel",)),
        name="post",
    )(h, y, p, wo, gf, wgu, wd, gp, wpg, wpu)


def _head_mean_square(x, ones_bd):
    return _dot(_bf16(x * x), ones_bd) * (1.0 / MB_HD)


def _norm_rope(x, ms, gain, cos_t, sin_lo, sin_hi):
    xn = x * lax.rsqrt(ms + EPS) * gain
    outs = []
    for s in range(x.shape[1] // LANES):
        xs = xn[:, s * LANES:(s + 1) * LANES]
        outs.append(xs * cos_t + pltpu.roll(xs, LANES - ROPE_HALF, 1) * sin_lo + pltpu.roll(xs, ROPE_HALF, 1) * sin_hi)
    return jnp.concatenate(outs, axis=1)


def _proj_kernel(h_ref, gq_ref, gkv_ref, wq_ref, wk_ref, wv_ref, qn_ref, kn_ref, bd_ref, cos_ref, slo_ref, shi_ref,
                 q_ref, k_ref, vt_ref, km_ref):
    r = _rms_rows(h_ref[...])
    hq = _bf16(r * gq_ref[...])
    hkv = _bf16(r * gkv_ref[...])
    tables = (cos_ref[...], slo_ref[...], shi_ref[...])
    bd = bd_ref[...]
    kvw = MB_KV_HEADS * MB_HD
    slab = lambda s: slice(s * kvw, (s + 1) * kvw)
    nq = D_MODEL // kvw
    raw = [_dot(hq, wq_ref[:, slab(s)]) for s in range(nq)] + [_dot(hkv, wk_ref[:, slab(s)]) for s in range(2)]
    gains = [qn_ref[:, slab(s)] for s in range(nq)] + [kn_ref[:, slab(s)] for s in range(2)]
    vs = _dot(hkv, wv_ref[...])
    means = [_head_mean_square(x, bd) for x in raw]
    vt_ref[0] = _bf16(vs.T)
    for s in range(nq):
        q_ref[:, slab(s)] = _bf16(_norm_rope(raw[s], means[s], gains[s], *tables))
    for s in range(2):
        ks = _norm_rope(raw[nq + s], means[nq + s], gains[nq + s], *tables)
        k_ref[:, slab(s)] = _bf16(ks)
        for b in range(ks.shape[0] // MB_BLOCK):
            km_ref[b, :, slab(s)] = jnp.mean(ks[b * MB_BLOCK:(b + 1) * MB_BLOCK], axis=0, keepdims=True)


def _proj(h, gq, gkv, wq, wk2, wv, qn, kn2, bd, cos_t, sin_lo, sin_hi, seq):
    t = h.shape[0]
    tm = ROW_TILE
    kvw = MB_KV_HEADS * MB_HD
    row = lambda n: pl.BlockSpec((tm, n), lambda i: (i, 0))
    tab = pl.BlockSpec((tm, LANES), lambda i: (i % (seq // tm), 0))
    return pl.pallas_call(
        _proj_kernel,
        grid=(t // tm,),
        in_specs=[row(D_MODEL), _const_spec((1, D_MODEL)), _const_spec((1, D_MODEL)), _const_spec(wq.shape),
                  _const_spec(wk2.shape), _const_spec(wv.shape), _const_spec((1, D_MODEL)), _const_spec((1, 2 * kvw)),
                  _const_spec(bd.shape), tab, tab, tab],
        out_specs=[row(D_MODEL), row(2 * kvw),
                   pl.BlockSpec((1, kvw, tm), lambda i: (i // (seq // tm), 0, i % (seq // tm))),
                   pl.BlockSpec((tm // MB_BLOCK, 1, 2 * kvw), lambda i: (i, 0, 0))],
        out_shape=[jax.ShapeDtypeStruct((t, D_MODEL), jnp.bfloat16),
                   jax.ShapeDtypeStruct((t, 2 * kvw), jnp.bfloat16),
                   jax.ShapeDtypeStruct((t // seq, kvw, seq), jnp.bfloat16),
                   jax.ShapeDtypeStruct((t // MB_BLOCK, 1, 2 * kvw), jnp.float32)],
        compiler_params=_params(("parallel",)),
        name="proj",
    )(h, gq, gkv, wq, wk2, wv, qn, kn2, bd, cos_t, sin_lo, sin_hi)


MB_GROUPS_PER_STEP = 2


def _moba_kernel(*refs, i):
    q_ref, k_ref, vt_ref, km_ref = refs[:4]
    o_ref, qa_ref, s_ref, acc_ref, vaug_ref = refs[-5:]
    nb = km_ref.shape[1]
    gw = MB_GROUP * MB_HD
    streams = [(gg, h) for gg in range(MB_GROUPS_PER_STEP) for h in range(MB_GROUP)]
    sid = lambda gg, h: gg * MB_GROUP + h
    lanes = lambda gg: slice(gg * LANES, (gg + 1) * LANES)
    nkeys = (i + 1) * MB_BLOCK
    depth = s_ref.shape[1]
    first_half = lax.broadcasted_iota(jnp.int32, (1, LANES), 1) < MB_HD
    zero = jnp.zeros((), jnp.bfloat16)
    for gg in range(MB_GROUPS_PER_STEP):
        for u in range(MB_GROUP // 2):
            t = q_ref[:, gg * gw + u * LANES:gg * gw + (u + 1) * LANES]
            qa_ref[sid(gg, 2 * u)] = jnp.where(first_half, t, zero)
            qa_ref[sid(gg, 2 * u + 1)] = jnp.where(first_half, zero, t)
        vaug_ref[gg, :MB_HD, :] = vt_ref[0, gg * MB_HD:(gg + 1) * MB_HD, :]
        vaug_ref[gg, MB_HD:, :] = jnp.ones((vaug_ref.shape[1] - MB_HD, nkeys), jnp.bfloat16)

    blk = lax.broadcasted_iota(jnp.int32, (nb, MB_BLOCK), 0)
    past = blk < i
    key_i = lax.broadcasted_iota(jnp.int32, (MB_BLOCK, MB_BLOCK), 0)
    qry_i = lax.broadcasted_iota(jnp.int32, (MB_BLOCK, MB_BLOCK), 1)
    causal = key_i <= qry_i

    def scores_of(n, gg):
        maxes = {}
        for h in range(MB_GROUP):
            s = _dot_nt(k_ref[0, n, :, lanes(gg)], qa_ref[sid(gg, h)])
            s_ref[sid(gg, h), n % depth] = s
            maxes[gg, h] = jnp.max(s, axis=0, keepdims=True)
        return maxes

    def weights(s, offset):
        return jnp.exp2(_bf16(s - offset))

    own, chosen = {}, {}
    for gg in range(MB_GROUPS_PER_STEP):
        km = km_ref[0, :, 0, lanes(gg)]
        km_hi = _bf16(km)
        km_lo = _bf16(km - km_hi.astype(jnp.float32))
        own_and_means = jnp.concatenate([k_ref[0, i, :, lanes(gg)], km_hi, km_lo], axis=0)
        for h in range(MB_GROUP):
            r = _dot_nt(own_and_means, qa_ref[sid(gg, h)])
            own[gg, h] = jnp.where(causal, r[:MB_BLOCK], NEG_BIG)
            g = jnp.where(past, r[MB_BLOCK:MB_BLOCK + nb] + r[MB_BLOCK + nb:], -jnp.inf)
            picked = jnp.zeros((nb, MB_BLOCK), jnp.bool_)
            for _ in range(MB_TOPK):
                top = jnp.max(g, axis=0, keepdims=True)
                first = jnp.min(jnp.where(g == top, blk, nb), axis=0, keepdims=True)
                pick = blk == first
                picked = picked | pick
                g = jnp.where(pick, -jnp.inf, g)
            chosen[gg, h] = picked & past
    block_max = {}
    if i > 0:
        for gg in range(MB_GROUPS_PER_STEP):
            block_max.update(scores_of(0, gg))

    ms, ls = {}, {}
    acc_rows = lambda gg, h: slice(sid(gg, h) * MB_HD, (sid(gg, h) + 1) * MB_HD)
    for gg, h in streams:
        mx = jnp.max(own[gg, h], axis=0, keepdims=True)
        pv = _dot(vaug_ref[gg, :, i * MB_BLOCK:nkeys], weights(own[gg, h], mx))
        ms[gg, h] = mx
        ls[gg, h] = pv[MB_HD:MB_HD + 1, :]
        acc_ref[acc_rows(gg, h), :] = pv[:MB_HD, :]

    for n in range(i):
        next_max = {}
        for gg in range(MB_GROUPS_PER_STEP):
            if n + 1 < i:
                next_max.update(scores_of(n + 1, gg))
            for h in range(MB_GROUP):
                use = chosen[gg, h][n:n + 1]
                m_new = jnp.maximum(ms[gg, h], jnp.where(use, block_max[gg, h], NEG_BIG))
                alpha = jnp.exp2(ms[gg, h] - m_new)
                p = weights(s_ref[sid(gg, h), n % depth], jnp.where(use, m_new, -NEG_BIG))
                pv = _dot(vaug_ref[gg, :, n * MB_BLOCK:(n + 1) * MB_BLOCK], p)
                ms[gg, h] = m_new
                ls[gg, h] = alpha * ls[gg, h] + pv[MB_HD:MB_HD + 1, :]
                acc_ref[acc_rows(gg, h), :] = alpha * acc_ref[acc_rows(gg, h), :] + pv[:MB_HD, :]
        block_max = next_max

    for gg, h in streams:
        acc_ref[acc_rows(gg, h), :] = acc_ref[acc_rows(gg, h), :] / ls[gg, h]
    for gg in range(MB_GROUPS_PER_STEP):
        o_ref[:, gg * gw:(gg + 1) * gw] = _bf16(acc_ref[gg * gw:(gg + 1) * gw, :].T)


def _moba(q, k, vt, km, batch, seq):
    nb = seq // MB_BLOCK
    t = batch * seq
    gs = MB_GROUPS_PER_STEP
    gw = gs * MB_GROUP * MB_HD
    n_streams = gs * MB_GROUP
    y = q
    for i in range(nb):
        nkeys = (i + 1) * MB_BLOCK
        y = pl.pallas_call(
            functools.partial(_moba_kernel, i=i),
            grid=(batch, MB_KV_HEADS // gs),
            in_specs=[pl.BlockSpec((MB_BLOCK, gw), lambda b, g, i=i: (b * nb + i, g)),
                      pl.BlockSpec((1, i + 1, MB_BLOCK, gs * LANES), lambda b, g: (b, 0, 0, g)),
                      pl.BlockSpec((1, gs * MB_HD, nkeys), lambda b, g: (b, g, 0)),
                      pl.BlockSpec((1, nb, 1, gs * LANES), lambda b, g: (b, 0, 0, g))],
            out_specs=pl.BlockSpec((MB_BLOCK, gw), lambda b, g, i=i: (b * nb + i, g)),
            out_shape=jax.ShapeDtypeStruct((t, MB_HEADS * MB_HD), jnp.bfloat16),
            input_output_aliases={0: 0},
            scratch_shapes=[pltpu.VMEM((n_streams, MB_BLOCK, LANES), jnp.bfloat16),
                            pltpu.VMEM((n_streams, min(max(i, 1), 3), MB_BLOCK, MB_BLOCK), jnp.float32),
                            pltpu.VMEM((n_streams * MB_HD, MB_BLOCK), jnp.float32),
                            pltpu.VMEM((gs, MB_HD + 16, nkeys), jnp.bfloat16)],
            compiler_params=_params(("parallel", "parallel")),
            name=f"moba{i}",
        )(y, k, vt, km)
    return y


def _rope_tables(seq):
    inv = ROPE_THETA ** (-np.arange(0, ROPE_DIM, 2, dtype=np.float64) / ROPE_DIM)
    ang = np.arange(seq, dtype=np.float64)[:, None] * inv[None, :]
    cos, sin = np.cos(ang), np.sin(ang)
    zeros = np.zeros((seq, MB_HD - ROPE_DIM))
    zh = np.zeros((seq, ROPE_HALF))
    cos_h = np.concatenate([cos, cos, np.ones_like(zeros)], axis=1)
    lo_h = np.concatenate([-sin, zh, zeros], axis=1)
    hi_h = np.concatenate([zh, sin, zeros], axis=1)
    two = lambda a: jnp.asarray(np.concatenate([a, a], axis=1), jnp.float32)
    return two(cos_h), two(lo_h), two(hi_h)


def kernel(x, p, norm_mix, a_w_in, a_b_gate, a_mh_gain, a_w_out, kv_norm, w_kv, k_norm, b_w_q, b_q_norm, b_w_o,
           norm_ffn, w_gate_up, w_down, norm_ple, w_ple_gate, w_ple_up):
    batch, seq, d = x.shape
    t = batch * seq
    f32 = jnp.float32
    x2 = x.reshape(t, d)
    p2 = p.reshape(p.shape[0], t, PLE_DIM)
    row = lambda a: a.reshape(1, -1).astype(f32)

    w_in = a_w_in[0]
    n_main = 2 * ML_QK_COLS + 2 * ML_V_COLS
    w_main = _bf16(w_in[:, :n_main])
    w_gate = _bf16(jnp.pad(w_in[:, n_main:], ((0, 0), (0, GATE_LANES - 2 * ML_HEADS))))
    b_gate = jnp.pad(a_b_gate[0].astype(f32), (0, GATE_LANES - 2 * ML_HEADS)).reshape(1, GATE_LANES)
    q, k, v, o, gcol, grow = _inproj(x2, row(norm_mix[0]), w_main, w_gate, b_gate, seq)
    y = _mlstm(q, k, v, o, gcol, grow, row(a_mh_gain[0]), batch, seq)
    h = _post(x2, y, p2, 0, _bf16(a_w_out[0]), row(norm_ffn[0]), _bf16(w_gate_up[0]), _bf16(w_down[0]),
              row(norm_ple[0]), _bf16(w_ple_gate[0]), _bf16(w_ple_up[0]))

    nb = seq // MB_BLOCK
    kvw = MB_KV_HEADS * MB_HD
    cos_t, sin_lo, sin_hi = _rope_tables(seq)
    head = lax.broadcasted_iota(jnp.int32, (kvw, kvw), 0) // MB_HD
    ones_bd = (head == head.T).astype(jnp.bfloat16)
    qn = row(jnp.tile(b_q_norm[0], MB_HEADS)) * MB_Q_SCALE
    kn2 = row(jnp.tile(k_norm, 2 * MB_KV_HEADS))
    w_k = _bf16(w_kv[:, :kvw]).reshape(d, MB_KV_HEADS, 1, MB_HD)
    w_k2 = jnp.broadcast_to(w_k, (d, MB_KV_HEADS, 2, MB_HD)).reshape(d, 2 * kvw)
    q, k2, vt, km2 = _proj(h, row(norm_mix[1]), row(kv_norm), _bf16(b_w_q[0]), w_k2, _bf16(w_kv[:, kvw:]), qn, kn2,
                           ones_bd, cos_t, sin_lo, sin_hi, seq)
    y = _moba(q, k2.reshape(batch, nb, MB_BLOCK, 2 * kvw), vt, km2.reshape(batch, nb, 1, 2 * kvw), batch, seq)
    h = _post(h, y, p2, 1, _bf16(b_w_o[0]), row(norm_ffn[1]), _bf16(w_gate_up[1]), _bf16(w_down[1]),
              row(norm_ple[1]), _bf16(w_ple_gate[1]), _bf16(w_ple_up[1]))
    return h.reshape(batch, seq, d)
```

```python
import functools
import math

import jax
import jax.numpy as jnp
from jax import lax
from jax.experimental import pallas as pl
from jax.experimental.pallas import tpu as pltpu
import numpy as np

D_MODEL = 1024
PLE_DIM = 256
EPS = 1e-6

ML_HEADS = 8
ML_DV = 128
ML_DQK = 64
GATE_CAP = 15.0
ML_QK_COLS = ML_HEADS * ML_DQK
ML_V_COLS = ML_HEADS * ML_DV
ML_CHUNK = 256

MB_HEADS = 16
MB_KV_HEADS = 4
MB_HD = 64
MB_GROUP = MB_HEADS // MB_KV_HEADS
MB_BLOCK = 256
MB_TOPK = 3
ROPE_THETA = 500000.0
ROPE_DIM = MB_HD // 4
ROPE_HALF = ROPE_DIM // 2
MB_Q_SCALE = MB_HD ** -0.5 * math.log2(math.e)

FFN_HIDDEN = 2816

LANES = 128
GATE_LANES = LANES
NEG_BIG = -1e30
VMEM_LIMIT = 56 * 1024 * 1024

ROW_TILE = 1024
POST_ROW_TILE = 512
FFN_HIDDEN_SPLITS = ((0, 1536), (1536, FFN_HIDDEN))


def _bf16(x):
    return x.astype(jnp.bfloat16)


def _dot(a, b):
    return jnp.dot(a, b, preferred_element_type=jnp.float32)


def _dot_nt(a, b):
    return lax.dot_general(a, b, (((1,), (1,)), ((), ())), preferred_element_type=jnp.float32)


def _dot_tn(a, b):
    return lax.dot_general(a, b, (((0,), (0,)), ((), ())), preferred_element_type=jnp.float32)


def _rms_rows(x):
    return x * lax.rsqrt(jnp.mean(x * x, axis=-1, keepdims=True) + EPS)


def _const_spec(shape):
    nd = len(shape)
    return pl.BlockSpec(shape, lambda *_: (0,) * nd, pipeline_mode=pl.Buffered(1))


def _params(sem):
    return pltpu.CompilerParams(dimension_semantics=sem, vmem_limit_bytes=VMEM_LIMIT)


def _inproj_kernel(x_ref, g_ref, w_ref, wg_ref, bg_ref, q_ref, k_ref, v_ref, o_ref, gate_ref, gate_t_ref):
    hn = _bf16(_rms_rows(x_ref[...]) * g_ref[...])
    c0, c1, c2, c3 = ML_QK_COLS, 2 * ML_QK_COLS, 2 * ML_QK_COLS + ML_V_COLS, 2 * ML_QK_COLS + 2 * ML_V_COLS
    pre = _dot(hn, wg_ref[...]) + bg_ref[...]
    cap = GATE_CAP * jnp.tanh(pre * (1.0 / GATE_CAP))
    logf = jnp.minimum(cap, 0.0) - jnp.log1p(jnp.exp(-jnp.abs(cap)))
    rows = cap.shape[0]
    row_in_chunk = lax.broadcasted_iota(jnp.int32, cap.shape, 0) % ML_CHUNK
    cum = logf
    d = 1
    while d < ML_CHUNK:
        cum = cum + jnp.where(row_in_chunk >= d, pltpu.roll(cum, d, 0), 0.0)
        d *= 2
    li_shift = pltpu.roll(cap, ML_HEADS, 1)
    parts = []
    for c in range(rows // ML_CHUNK):
        sl = slice(c * ML_CHUNK, (c + 1) * ML_CHUNK)
        g_last = cum[(c + 1) * ML_CHUNK - 1:(c + 1) * ML_CHUNK, :]
        parts.append(g_last - cum[sl] + li_shift[sl])
    wlog = pltpu.roll(jnp.concatenate(parts, axis=0), ML_HEADS, 1)
    lane = lax.broadcasted_iota(jnp.int32, cap.shape, 1)
    gates = jnp.where(lane < ML_HEADS, cap, jnp.where(lane < 2 * ML_HEADS, cum, wlog))
    gate_ref[...] = gates
    gate_t_ref[0] = gates.T

    q_ref[...] = _bf16(_dot(hn, w_ref[:, 0:c0]) * (ML_DQK ** -0.5))
    k_ref[...] = _bf16(_dot(hn, w_ref[:, c0:c1]))
    v_ref[...] = _bf16(_dot(hn, w_ref[:, c1:c2]))
    o_ref[...] = _bf16(_dot(hn, w_ref[:, c2:c3]))


def _inproj(x2, g, w, wg, bg, seq):
    t = x2.shape[0]
    tm = ROW_TILE
    row = lambda n: pl.BlockSpec((tm, n), lambda i: (i, 0))
    return pl.pallas_call(
        _inproj_kernel,
        grid=(t // tm,),
        in_specs=[row(D_MODEL), _const_spec((1, D_MODEL)), _const_spec(w.shape), _const_spec(wg.shape),
                  _const_spec((1, GATE_LANES))],
        out_specs=[row(ML_QK_COLS), row(ML_QK_COLS), row(ML_V_COLS), row(ML_V_COLS), row(GATE_LANES),
                   pl.BlockSpec((1, GATE_LANES, tm), lambda i: (i // (seq // tm), 0, i % (seq // tm)))],
        out_shape=[jax.ShapeDtypeStruct((t, ML_QK_COLS), jnp.bfloat16),
                   jax.ShapeDtypeStruct((t, ML_QK_COLS), jnp.bfloat16),
                   jax.ShapeDtypeStruct((t, ML_V_COLS), jnp.bfloat16),
                   jax.ShapeDtypeStruct((t, ML_V_COLS), jnp.bfloat16),
                   jax.ShapeDtypeStruct((t, GATE_LANES), jnp.float32),
                   jax.ShapeDtypeStruct((t // seq, GATE_LANES, seq), jnp.float32)],
        compiler_params=_params(("parallel",)),
        name="inproj",
    )(x2, g, w, wg, bg)


def _mlstm_kernel(q_ref, k_ref, v_ref, o_ref, gcol_ref, grow_ref, gain_ref, y_ref, c_ref):
    L = ML_CHUNK

    @pl.when(pl.program_id(1) == 0)
    def _():
        c_ref[...] = jnp.zeros_like(c_ref)

    lane_row = lax.broadcasted_iota(jnp.int32, (1, LANES), 1)
    r_i = lax.broadcasted_iota(jnp.int32, (L, L), 0)
    c_i = lax.broadcasted_iota(jnp.int32, (L, L), 1)
    causal = r_i >= c_i
    gc = gcol_ref[...]
    egc = jnp.exp(gc)
    gain = gain_ref[...]
    heads = range(ML_HEADS)
    pair = lambda h: slice((h // 2) * LANES, (h // 2 + 1) * LANES)
    cols = lambda h: slice(h * ML_DV, (h + 1) * ML_DV)
    col = lambda a, c: a[:, c:c + 1]

    ones = jnp.ones((L, ML_DV), jnp.bfloat16)
    vs = [jnp.concatenate([v_ref[:, cols(h)], ones], axis=1) for h in heads]

    scores, inter = [], []
    for h in heads:
        head_lanes = (lane_row // ML_DQK) == (h % 2)
        q = jnp.where(head_lanes, q_ref[:, pair(h)], jnp.zeros((), jnp.bfloat16))
        scores.append(_dot_nt(q, k_ref[:, pair(h)]))
        inter.append(_dot(q, _bf16(c_ref[h])))

    ps = []
    for h in heads:
        c_row = grow_ref[0, h:h + 1, :] - grow_ref[0, ML_HEADS + h:ML_HEADS + h + 1, :]
        dmat = jnp.where(causal, col(gc, ML_HEADS + h) + c_row, NEG_BIG)
        ps.append(_bf16(scores[h] * jnp.exp(dmat)))

    for h in heads:
        both = _dot(ps[h], vs[h]) + inter[h] * col(egc, ML_HEADS + h)
        hc = both[:, :ML_DV] / jnp.maximum(jnp.abs(both[:, ML_DV:]), 1.0)
        hn = _rms_rows(hc) * gain
        y_ref[:, cols(h)] = _bf16(jax.nn.sigmoid(o_ref[:, cols(h)].astype(jnp.float32)) * hn)

    for h in heads:
        decay = jnp.exp(grow_ref[0, ML_HEADS + h:ML_HEADS + h + 1, L - 1:L])
        kw = k_ref[:, pair(h)].astype(jnp.float32) * col(egc, 2 * ML_HEADS + h)
        c_ref[h] = decay * c_ref[h] + _dot_tn(_bf16(kw), vs[h])


def _mlstm(q, k, v, o, gcol, grow, gain, batch, seq):
    L = ML_CHUNK
    nc = seq // L
    t = batch * seq
    row = lambda n: pl.BlockSpec((L, n), lambda b, c: (b * nc + c, 0))
    return pl.pallas_call(
        _mlstm_kernel,
        grid=(batch, nc),
        in_specs=[row(ML_QK_COLS), row(ML_QK_COLS), row(ML_V_COLS), row(ML_V_COLS), row(GATE_LANES),
                  pl.BlockSpec((1, 3 * ML_HEADS, L), lambda b, c: (b, 0, c)),
                  pl.BlockSpec((1, ML_DV), lambda b, c: (0, 0))],
        out_specs=row(ML_V_COLS),
        out_shape=jax.ShapeDtypeStruct((t, ML_V_COLS), jnp.bfloat16),
        scratch_shapes=[pltpu.VMEM((ML_HEADS, LANES, 2 * ML_DV), jnp.float32)],
        compiler_params=_params(("parallel", "arbitrary")),
        name="mlstm",
    )(q, k, v, o, gcol, grow, gain)


def _post_kernel(h_ref, y_ref, p_ref, wo_ref, gf_ref, wgu_ref, wd_ref, gp_ref, wpg_ref, wpu_ref, out_ref):
    h1 = h_ref[...] + _dot(y_ref[...], wo_ref[...])
    hn = _bf16(_rms_rows(h1) * gf_ref[...])
    h2 = h1
    for lo, hi in FFN_HIDDEN_SPLITS:
        gate_pre = _dot(hn, wgu_ref[:, lo:hi])
        up_pre = _dot(hn, wgu_ref[:, FFN_HIDDEN + lo:FFN_HIDDEN + hi])
        h2 = h2 + _dot(_bf16(jax.nn.silu(gate_pre) * up_pre), wd_ref[lo:hi, :])
    hp = _bf16(_rms_rows(h2) * gp_ref[...])
    gate = jax.nn.sigmoid(_dot(hp, wpg_ref[...]))
    up = _dot(_bf16(p_ref[...]), wpu_ref[...])
    out_ref[...] = h2 + up * gate


def _post(h, y, p, layer, wo, gf, wgu, wd, gp, wpg, wpu):
    t = h.shape[0]
    tm = POST_ROW_TILE
    row = lambda n: pl.BlockSpec((tm, n), lambda i: (i, 0))
    p_spec = pl.BlockSpec((None, tm, PLE_DIM), lambda i: (layer, i, 0))
    return pl.pallas_call(
        _post_kernel,
        grid=(t // tm,),
        in_specs=[row(D_MODEL), row(D_MODEL), p_spec, _const_spec(wo.shape), _const_spec((1, D_MODEL)),
                  _const_spec(wgu.shape), _const_spec(wd.shape), _const_spec((1, D_MODEL)),
                  _const_spec(wpg.shape), _const_spec(wpu.shape)],
        out_specs=row(D_MODEL),
        out_shape=jax.ShapeDtypeStruct((t, D_MODEL), jnp.float32),
        compiler_params=_params(("parallel",)),
        name="post",
    )(h, y, p, wo, gf, wgu, wd, gp, wpg, wpu)


def _head_mean_square(x, ones_bd):
    return _dot(_bf16(x * x), ones_bd) * (1.0 / MB_HD)


def _norm_rope(x, ms, gain, cos_t, sin_lo, sin_hi):
    xn = x * lax.rsqrt(ms + EPS) * gain
    outs = []
    for s in range(x.shape[1] // LANES):
        xs = xn[:, s * LANES:(s + 1) * LANES]
        outs.append(xs * cos_t + pltpu.roll(xs, LANES - ROPE_HALF, 1) * sin_lo + pltpu.roll(xs, ROPE_HALF, 1) * sin_hi)
    return jnp.concatenate(outs, axis=1)


def _proj_kernel(h_ref, gq_ref, gkv_ref, wq_ref, wk_ref, wv_ref, qn_ref, kn_ref, bd_ref, cos_ref, slo_ref, shi_ref,
                 q_ref, k_ref, vt_ref, km_ref):
    r = _rms_rows(h_ref[...])
    hq = _bf16(r * gq_ref[...])
    hkv = _bf16(r * gkv_ref[...])
    tables = (cos_ref[...], slo_ref[...], shi_ref[...])
    bd = bd_ref[...]
    kvw = MB_KV_HEADS * MB_HD
    slab = lambda s: slice(s * kvw, (s + 1) * kvw)
    nq = D_MODEL // kvw
    raw = [_dot(hq, wq_ref[:, slab(s)]) for s in range(nq)] + [_dot(hkv, wk_ref[:, slab(s)]) for s in range(2)]
    gains = [qn_ref[:, slab(s)] for s in range(nq)] + [kn_ref[:, slab(s)] for s in range(2)]
    vs = _dot(hkv, wv_ref[...])
    means = [_head_mean_square(x, bd) for x in raw]
    vt_ref[0] = _bf16(vs.T)
    for s in range(nq):
        q_ref[:, slab(s)] = _bf16(_norm_rope(raw[s], means[s], gains[s], *tables))
    for s in range(2):
        ks = _norm_rope(raw[nq + s], means[nq + s], gains[nq + s], *tables)
        k_ref[:, slab(s)] = _bf16(ks)
        for b in range(ks.shape[0] // MB_BLOCK):
            km_ref[b, :, slab(s)] = jnp.mean(ks[b * MB_BLOCK:(b + 1) * MB_BLOCK], axis=0, keepdims=True)


def _proj(h, gq, gkv, wq, wk2, wv, qn, kn2, bd, cos_t, sin_lo, sin_hi, seq):
    t = h.shape[0]
    tm = ROW_TILE
    kvw = MB_KV_HEADS * MB_HD
    row = lambda n: pl.BlockSpec((tm, n), lambda i: (i, 0))
    tab = pl.BlockSpec((tm, LANES), lambda i: (i % (seq // tm), 0))
    return pl.pallas_call(
        _proj_kernel,
        grid=(t // tm,),
        in_specs=[row(D_MODEL), _const_spec((1, D_MODEL)), _const_spec((1, D_MODEL)), _const_spec(wq.shape),
                  _const_spec(wk2.shape), _const_spec(wv.shape), _const_spec((1, D_MODEL)), _const_spec((1, 2 * kvw)),
                  _const_spec(bd.shape), tab, tab, tab],
        out_specs=[row(D_MODEL), row(2 * kvw),
                   pl.BlockSpec((1, kvw, tm), lambda i: (i // (seq // tm), 0, i % (seq // tm))),
                   pl.BlockSpec((tm // MB_BLOCK, 1, 2 * kvw), lambda i: (i, 0, 0))],
        out_shape=[jax.ShapeDtypeStruct((t, D_MODEL), jnp.bfloat16),
                   jax.ShapeDtypeStruct((t, 2 * kvw), jnp.bfloat16),
                   jax.ShapeDtypeStruct((t // seq, kvw, seq), jnp.bfloat16),
                   jax.ShapeDtypeStruct((t // MB_BLOCK, 1, 2 * kvw), jnp.float32)],
        compiler_params=_params(("parallel",)),
        name="proj",
    )(h, gq, gkv, wq, wk2, wv, qn, kn2, bd, cos_t, sin_lo, sin_hi)


MB_GROUPS_PER_STEP = 2


def _moba_kernel(*refs, i):
    q_ref, k_ref, vt_ref, km_ref = refs[:4]
    o_ref, qa_ref, s_ref, acc_ref, vaug_ref = refs[-5:]
    nb = km_ref.shape[1]
    gw = MB_GROUP * MB_HD
    streams = [(gg, h) for gg in range(MB_GROUPS_PER_STEP) for h in range(MB_GROUP)]
    sid = lambda gg, h: gg * MB_GROUP + h
    lanes = lambda gg: slice(gg * LANES, (gg + 1) * LANES)
    nkeys = (i + 1) * MB_BLOCK
    depth = s_ref.shape[1]
    first_half = lax.broadcasted_iota(jnp.int32, (1, LANES), 1) < MB_HD
    zero = jnp.zeros((), jnp.bfloat16)
    for gg in range(MB_GROUPS_PER_STEP):
        for u in range(MB_GROUP // 2):
            t = q_ref[:, gg * gw + u * LANES:gg * gw + (u + 1) * LANES]
            qa_ref[sid(gg, 2 * u)] = jnp.where(first_half, t, zero)
            qa_ref[sid(gg, 2 * u + 1)] = jnp.where(first_half, zero, t)
        vaug_ref[gg, :MB_HD, :] = vt_ref[0, gg * MB_HD:(gg + 1) * MB_HD, :]
        vaug_ref[gg, MB_HD:, :] = jnp.ones((vaug_ref.shape[1] - MB_HD, nkeys), jnp.bfloat16)

    blk = lax.broadcasted_iota(jnp.int32, (nb, MB_BLOCK), 0)
    past = blk < i
    key_i = lax.broadcasted_iota(jnp.int32, (MB_BLOCK, MB_BLOCK), 0)
    qry_i = lax.broadcasted_iota(jnp.int32, (MB_BLOCK, MB_BLOCK), 1)
    causal = key_i <= qry_i

    def scores_of(n, gg):
        maxes = {}
        for h in range(MB_GROUP):
            s = _dot_nt(k_ref[0, n, :, lanes(gg)], qa_ref[sid(gg, h)])
            s_ref[sid(gg, h), n % depth] = s
            maxes[gg, h] = jnp.max(s, axis=0, keepdims=True)
        return maxes

    def weights(s, offset):
        return jnp.exp2(_bf16(s - offset))

    own, chosen = {}, {}
    for gg in range(MB_GROUPS_PER_STEP):
        km = km_ref[0, :, 0, lanes(gg)]
        km_hi = _bf16(km)
        km_lo = _bf16(km - km_hi.astype(jnp.float32))
        own_and_means = jnp.concatenate([k_ref[0, i, :, lanes(gg)], km_hi, km_lo], axis=0)
        for h in range(MB_GROUP):
            r = _dot_nt(own_and_means, qa_ref[sid(gg, h)])
            own[gg, h] = jnp.where(causal, r[:MB_BLOCK], NEG_BIG)
            g = jnp.where(past, r[MB_BLOCK:MB_BLOCK + nb] + r[MB_BLOCK + nb:], -jnp.inf)
            picked = jnp.zeros((nb, MB_BLOCK), jnp.bool_)
            for _ in range(MB_TOPK):
                top = jnp.max(g, axis=0, keepdims=True)
                first = jnp.min(jnp.where(g == top, blk, nb), axis=0, keepdims=True)
                pick = blk == first
                picked = picked | pick
                g = jnp.where(pick, -jnp.inf, g)
            chosen[gg, h] = picked & past
    block_max = {}
    if i > 0:
        for gg in range(MB_GROUPS_PER_STEP):
            block_max.update(scores_of(0, gg))

    ms, ls = {}, {}
    acc_rows = lambda gg, h: slice(sid(gg, h) * MB_HD, (sid(gg, h) + 1) * MB_HD)
    for gg, h in streams:
        mx = jnp.max(own[gg, h], axis=0, keepdims=True)
        pv = _dot(vaug_ref[gg, :, i * MB_BLOCK:nkeys], weights(own[gg, h], mx))
        ms[gg, h] = mx
        ls[gg, h] = pv[MB_HD:MB_HD + 1, :]
        acc_ref[acc_rows(gg, h), :] = pv[:MB_HD, :]

    for n in range(i):
        next_max = {}
        for gg in range(MB_GROUPS_PER_STEP):
            if n + 1 < i:
                next_max.update(scores_of(n + 1, gg))
            for h in range(MB_GROUP):
                use = chosen[gg, h][n:n + 1]
                m_new = jnp.maximum(ms[gg, h], jnp.where(use, block_max[gg, h], NEG_BIG))
                alpha = jnp.exp2(ms[gg, h] - m_new)
                p = weights(s_ref[sid(gg, h), n % depth], jnp.where(use, m_new, -NEG_BIG))
                pv = _dot(vaug_ref[gg, :, n * MB_BLOCK:(n + 1) * MB_BLOCK], p)
                ms[gg, h] = m_new
                ls[gg, h] = alpha * ls[gg, h] + pv[MB_HD:MB_HD + 1, :]
                acc_ref[acc_rows(gg, h), :] = alpha * acc_ref[acc_rows(gg, h), :] + pv[:MB_HD, :]
        block_max = next_max

    for gg, h in streams:
        acc_ref[acc_rows(gg, h), :] = acc_ref[acc_rows(gg, h), :] / ls[gg, h]
    for gg in range(MB_GROUPS_PER_STEP):
        o_ref[:, gg * gw:(gg + 1) * gw] = _bf16(acc_ref[gg * gw:(gg + 1) * gw, :].T)


def _moba(q, k, vt, km, batch, seq):
    nb = seq // MB_BLOCK
    t = batch * seq
    gs = MB_GROUPS_PER_STEP
    gw = gs * MB_GROUP * MB_HD
    n_streams = gs * MB_GROUP
    y = q
    for i in range(nb):
        nkeys = (i + 1) * MB_BLOCK
        y = pl.pallas_call(
            functools.partial(_moba_kernel, i=i),
            grid=(batch, MB_KV_HEADS // gs),
            in_specs=[pl.BlockSpec((MB_BLOCK, gw), lambda b, g, i=i: (b * nb + i, g)),
                      pl.BlockSpec((1, i + 1, MB_BLOCK, gs * LANES), lambda b, g: (b, 0, 0, g)),
                      pl.BlockSpec((1, gs * MB_HD, nkeys), lambda b, g: (b, g, 0)),
                      pl.BlockSpec((1, nb, 1, gs * LANES), lambda b, g: (b, 0, 0, g))],
            out_specs=pl.BlockSpec((MB_BLOCK, gw), lambda b, g, i=i: (b * nb + i, g)),
            out_shape=jax.ShapeDtypeStruct((t, MB_HEADS * MB_HD), jnp.bfloat16),
            input_output_aliases={0: 0},
            scratch_shapes=[pltpu.VMEM((n_streams, MB_BLOCK, LANES), jnp.bfloat16),
                            pltpu.VMEM((n_streams, min(max(i, 1), 3), MB_BLOCK, MB_BLOCK), jnp.float32),
                            pltpu.VMEM((n_streams * MB_HD, MB_BLOCK), jnp.float32),
                            pltpu.VMEM((gs, MB_HD + 16, nkeys), jnp.bfloat16)],
            compiler_params=_params(("parallel", "parallel")),
            name=f"moba{i}",
        )(y, k, vt, km)
    return y


def _rope_tables(seq):
    inv = ROPE_THETA ** (-np.arange(0, ROPE_DIM, 2, dtype=np.float64) / ROPE_DIM)
    ang = np.arange(seq, dtype=np.float64)[:, None] * inv[None, :]
    cos, sin = np.cos(ang), np.sin(ang)
    zeros = np.zeros((seq, MB_HD - ROPE_DIM))
    zh = np.zeros((seq, ROPE_HALF))
    cos_h = np.concatenate([cos, cos, np.ones_like(zeros)], axis=1)
    lo_h = np.concatenate([-sin, zh, zeros], axis=1)
    hi_h = np.concatenate([zh, sin, zeros], axis=1)
    two = lambda a: jnp.asarray(np.concatenate([a, a], axis=1), jnp.float32)
    return two(cos_h), two(lo_h), two(hi_h)


def kernel(x, p, norm_mix, a_w_in, a_b_gate, a_mh_gain, a_w_out, kv_norm, w_kv, k_norm, b_w_q, b_q_norm, b_w_o,
           norm_ffn, w_gate_up, w_down, norm_ple, w_ple_gate, w_ple_up):
    batch, seq, d = x.shape
    t = batch * seq
    f32 = jnp.float32
    x2 = x.reshape(t, d)
    p2 = p.reshape(p.shape[0], t, PLE_DIM)
    row = lambda a: a.reshape(1, -1).astype(f32)

    w_in = a_w_in[0]
    n_main = 2 * ML_QK_COLS + 2 * ML_V_COLS
    w_main = _bf16(w_in[:, :n_main])
    w_gate = _bf16(jnp.pad(w_in[:, n_main:], ((0, 0), (0, GATE_LANES - 2 * ML_HEADS))))
    b_gate = jnp.pad(a_b_gate[0].astype(f32), (0, GATE_LANES - 2 * ML_HEADS)).reshape(1, GATE_LANES)
    q, k, v, o, gcol, grow = _inproj(x2, row(norm_mix[0]), w_main, w_gate, b_gate, seq)
    y = _mlstm(q, k, v, o, gcol, grow, row(a_mh_gain[0]), batch, seq)
    h = _post(x2, y, p2, 0, _bf16(a_w_out[0]), row(norm_ffn[0]), _bf16(w_gate_up[0]), _bf16(w_down[0]),
              row(norm_ple[0]), _bf16(w_ple_gate[0]), _bf16(w_ple_up[0]))

    nb = seq // MB_BLOCK
    kvw = MB_KV_HEADS * MB_HD
    cos_t, sin_lo, sin_hi = _rope_tables(seq)
    head = lax.broadcasted_iota(jnp.int32, (kvw, kvw), 0) // MB_HD
    ones_bd = (head == head.T).astype(jnp.bfloat16)
    qn = row(jnp.tile(b_q_norm[0], MB_HEADS)) * MB_Q_SCALE
    kn2 = row(jnp.tile(k_norm, 2 * MB_KV_HEADS))
    w_k = _bf16(w_kv[:, :kvw]).reshape(d, MB_KV_HEADS, 1, MB_HD)
    w_k2 = jnp.broadcast_to(w_k, (d, MB_KV_HEADS, 2, MB_HD)).reshape(d, 2 * kvw)
    q, k2, vt, km2 = _proj(h, row(norm_mix[1]), row(kv_norm), _bf16(b_w_q[0]), w_k2, _bf16(w_kv[:, kvw:]), qn, kn2,
                           ones_bd, cos_t, sin_lo, sin_hi, seq)
    y = _moba(q, k2.reshape(batch, nb, MB_BLOCK, 2 * kvw), vt, km2.reshape(batch, nb, 1, 2 * kvw), batch, seq)
    h = _post(h, y, p2, 1, _bf16(b_w_o[0]), row(norm_ffn[1]), _bf16(w_gate_up[1]), _bf16(w_down[1]),
              row(norm_ple[1]), _bf16(w_ple_gate[1]), _bf16(w_ple_up[1]))
    return h.reshape(batch, seq, d)
```

```python
import functools
import math

import jax
import jax.numpy as jnp
from jax import lax
from jax.experimental import pallas as pl
from jax.experimental.pallas import tpu as pltpu
import numpy as np

D_MODEL = 1024
PLE_DIM = 256
EPS = 1e-6

ML_HEADS = 8
ML_DV = 128
ML_DQK = 64
GATE_CAP = 15.0
ML_QK_COLS = ML_HEADS * ML_DQK
ML_V_COLS = ML_HEADS * ML_DV
ML_CHUNK = 256

MB_HEADS = 16
MB_KV_HEADS = 4
MB_HD = 64
MB_GROUP = MB_HEADS // MB_KV_HEADS
MB_BLOCK = 256
MB_TOPK = 3
ROPE_THETA = 500000.0
ROPE_DIM = MB_HD // 4
ROPE_HALF = ROPE_DIM // 2
MB_Q_SCALE = MB_HD ** -0.5 * math.log2(math.e)

FFN_HIDDEN = 2816

LANES = 128
GATE_LANES = LANES
NEG_BIG = -1e30
VMEM_LIMIT = 56 * 1024 * 1024

ROW_TILE = 1024
POST_ROW_TILE = 512
FFN_HIDDEN_SPLITS = ((0, 1536), (1536, FFN_HIDDEN))


def _bf16(x):
    return x.astype(jnp.bfloat16)


def _dot(a, b):
    return jnp.dot(a, b, preferred_element_type=jnp.float32)


def _dot_nt(a, b):
    return lax.dot_general(a, b, (((1,), (1,)), ((), ())), preferred_element_type=jnp.float32)


def _dot_tn(a, b):
    return lax.dot_general(a, b, (((0,), (0,)), ((), ())), preferred_element_type=jnp.float32)


def _rms_rows(x):
    return x * lax.rsqrt(jnp.mean(x * x, axis=-1, keepdims=True) + EPS)


def _const_spec(shape):
    nd = len(shape)
    return pl.BlockSpec(shape, lambda *_: (0,) * nd, pipeline_mode=pl.Buffered(1))


def _params(sem):
    return pltpu.CompilerParams(dimension_semantics=sem, vmem_limit_bytes=VMEM_LIMIT)


def _inproj_kernel(x_ref, g_ref, w_ref, wg_ref, bg_ref, q_ref, k_ref, v_ref, o_ref, gate_ref, gate_t_ref):
    hn = _bf16(_rms_rows(x_ref[...]) * g_ref[...])
    c0, c1, c2, c3 = ML_QK_COLS, 2 * ML_QK_COLS, 2 * ML_QK_COLS + ML_V_COLS, 2 * ML_QK_COLS + 2 * ML_V_COLS
    pre = _dot(hn, wg_ref[...]) + bg_ref[...]
    cap = GATE_CAP * jnp.tanh(pre * (1.0 / GATE_CAP))
    logf = jnp.minimum(cap, 0.0) - jnp.log1p(jnp.exp(-jnp.abs(cap)))
    rows = cap.shape[0]
    row_in_chunk = lax.broadcasted_iota(jnp.int32, cap.shape, 0) % ML_CHUNK
    cum = logf
    d = 1
    while d < ML_CHUNK:
        cum = cum + jnp.where(row_in_chunk >= d, pltpu.roll(cum, d, 0), 0.0)
        d *= 2
    li_shift = pltpu.roll(cap, ML_HEADS, 1)
    parts = []
    for c in range(rows // ML_CHUNK):
        sl = slice(c * ML_CHUNK, (c + 1) * ML_CHUNK)
        g_last = cum[(c + 1) * ML_CHUNK - 1:(c + 1) * ML_CHUNK, :]
        parts.append(g_last - cum[sl] + li_shift[sl])
    wlog = pltpu.roll(jnp.concatenate(parts, axis=0), ML_HEADS, 1)
    lane = lax.broadcasted_iota(jnp.int32, cap.shape, 1)
    gates = jnp.where(lane < ML_HEADS, cap, jnp.where(lane < 2 * ML_HEADS, cum, wlog))
    gate_ref[...] = gates
    gate_t_ref[0] = gates.T

    q_ref[...] = _bf16(_dot(hn, w_ref[:, 0:c0]) * (ML_DQK ** -0.5))
    k_ref[...] = _bf16(_dot(hn, w_ref[:, c0:c1]))
    v_ref[...] = _bf16(_dot(hn, w_ref[:, c1:c2]))
    o_ref[...] = _bf16(_dot(hn, w_ref[:, c2:c3]))


def _inproj(x2, g, w, wg, bg, seq):
    t = x2.shape[0]
    tm = ROW_TILE
    row = lambda n: pl.BlockSpec((tm, n), lambda i: (i, 0))
    return pl.pallas_call(
        _inproj_kernel,
        grid=(t // tm,),
        in_specs=[row(D_MODEL), _const_spec((1, D_MODEL)), _const_spec(w.shape), _const_spec(wg.shape),
                  _const_spec((1, GATE_LANES))],
        out_specs=[row(ML_QK_COLS), row(ML_QK_COLS), row(ML_V_COLS), row(ML_V_COLS), row(GATE_LANES),
                   pl.BlockSpec((1, GATE_LANES, tm), lambda i: (i // (seq // tm), 0, i % (seq // tm)))],
        out_shape=[jax.ShapeDtypeStruct((t, ML_QK_COLS), jnp.bfloat16),
                   jax.ShapeDtypeStruct((t, ML_QK_COLS), jnp.bfloat16),
                   jax.ShapeDtypeStruct((t, ML_V_COLS), jnp.bfloat16),
                   jax.ShapeDtypeStruct((t, ML_V_COLS), jnp.bfloat16),
                   jax.ShapeDtypeStruct((t, GATE_LANES), jnp.float32),
                   jax.ShapeDtypeStruct((t // seq, GATE_LANES, seq), jnp.float32)],
        compiler_params=_params(("parallel",)),
        name="inproj",
    )(x2, g, w, wg, bg)


def _mlstm_kernel(q_ref, k_ref, v_ref, o_ref, gcol_ref, grow_ref, gain_ref, y_ref, c_ref):
    L = ML_CHUNK

    @pl.when(pl.program_id(1) == 0)
    def _():
        c_ref[...] = jnp.zeros_like(c_ref)

    lane_row = lax.broadcasted_iota(jnp.int32, (1, LANES), 1)
    r_i = lax.broadcasted_iota(jnp.int32, (L, L), 0)
    c_i = lax.broadcasted_iota(jnp.int32, (L, L), 1)
    causal = r_i >= c_i
    gc = gcol_ref[...]
    egc = jnp.exp(gc)
    gain = gain_ref[...]
    heads = range(ML_HEADS)
    pair = lambda h: slice((h // 2) * LANES, (h // 2 + 1) * LANES)
    cols = lambda h: slice(h * ML_DV, (h + 1) * ML_DV)
    col = lambda a, c: a[:, c:c + 1]

    ones = jnp.ones((L, ML_DV), jnp.bfloat16)
    vs = [jnp.concatenate([v_ref[:, cols(h)], ones], axis=1) for h in heads]

    scores, inter = [], []
    for h in heads:
        head_lanes = (lane_row // ML_DQK) == (h % 2)
        q = jnp.where(head_lanes, q_ref[:, pair(h)], jnp.zeros((), jnp.bfloat16))
        scores.append(_dot_nt(q, k_ref[:, pair(h)]))
        inter.append(_dot(q, _bf16(c_ref[h])))

    ps = []
    for h in heads:
        c_row = grow_ref[0, h:h + 1, :] - grow_ref[0, ML_HEADS + h:ML_HEADS + h + 1, :]
        dmat = jnp.where(causal, col(gc, ML_HEADS + h) + c_row, NEG_BIG)
        ps.append(_bf16(scores[h] * jnp.exp(dmat)))

    for h in heads:
        both = _dot(ps[h], vs[h]) + inter[h] * col(egc, ML_HEADS + h)
        hc = both[:, :ML_DV] / jnp.maximum(jnp.abs(both[:, ML_DV:]), 1.0)
        hn = _rms_rows(hc) * gain
        y_ref[:, cols(h)] = _bf16(jax.nn.sigmoid(o_ref[:, cols(h)].astype(jnp.float32)) * hn)

    for h in heads:
        decay = jnp.exp(grow_ref[0, ML_HEADS + h:ML_HEADS + h + 1, L - 1:L])
        kw = k_ref[:, pair(h)].astype(jnp.float32) * col(egc, 2 * ML_HEADS + h)
        c_ref[h] = decay * c_ref[h] + _dot_tn(_bf16(kw), vs[h])


def _mlstm(q, k, v, o, gcol, grow, gain, batch, seq):
    L = ML_CHUNK
    nc = seq // L
    t = batch * seq
    row = lambda n: pl.BlockSpec((L, n), lambda b, c: (b * nc + c, 0))
    return pl.pallas_call(
        _mlstm_kernel,
        grid=(batch, nc),
        in_specs=[row(ML_QK_COLS), row(ML_QK_COLS), row(ML_V_COLS), row(ML_V_COLS), row(GATE_LANES),
                  pl.BlockSpec((1, 3 * ML_HEADS, L), lambda b, c: (b, 0, c)),
                  pl.BlockSpec((1, ML_DV), lambda b, c: (0, 0))],
        out_specs=row(ML_V_COLS),
        out_shape=jax.ShapeDtypeStruct((t, ML_V_COLS), jnp.bfloat16),
        scratch_shapes=[pltpu.VMEM((ML_HEADS, LANES, 2 * ML_DV), jnp.float32)],
        compiler_params=_params(("parallel", "arbitrary")),
        name="mlstm",
    )(q, k, v, o, gcol, grow, gain)


def _post_kernel(h_ref, y_ref, p_ref, wo_ref, gf_ref, wgu_ref, wd_ref, gp_ref, wpg_ref, wpu_ref, out_ref):
    h1 = h_ref[...] + _dot(y_ref[...], wo_ref[...])
    hn = _bf16(_rms_rows(h1) * gf_ref[...])
    h2 = h1
    for lo, hi in FFN_HIDDEN_SPLITS:
        gate_pre = _dot(hn, wgu_ref[:, lo:hi])
        up_pre = _dot(hn, wgu_ref[:, FFN_HIDDEN + lo:FFN_HIDDEN + hi])
        h2 = h2 + _dot(_bf16(jax.nn.silu(gate_pre) * up_pre), wd_ref[lo:hi, :])
    hp = _bf16(_rms_rows(h2) * gp_ref[...])
    gate = jax.nn.sigmoid(_dot(hp, wpg_ref[...]))
    up = _dot(_bf16(p_ref[...]), wpu_ref[...])
    out_ref[...] = h2 + up * gate


def _post(h, y, p, layer, wo, gf, wgu, wd, gp, wpg, wpu):
    t = h.shape[0]
    tm = POST_ROW_TILE
    row = lambda n: pl.BlockSpec((tm, n), lambda i: (i, 0))
    p_spec = pl.BlockSpec((None, tm, PLE_DIM), lambda i: (layer, i, 0))
    return pl.pallas_call(
        _post_kernel,
        grid=(t // tm,),
        in_specs=[row(D_MODEL), row(D_MODEL), p_spec, _const_spec(wo.shape), _const_spec((1, D_MODEL)),
                  _const_spec(wgu.shape), _const_spec(wd.shape), _const_spec((1, D_MODEL)),
                  _const_spec(wpg.shape), _const_spec(wpu.shape)],
        out_specs=row(D_MODEL),
        out_shape=jax.ShapeDtypeStruct((t, D_MODEL), jnp.float32),
        compiler_params=_params(("parallel",)),
        name="post",
    )(h, y, p, wo, gf, wgu, wd, gp, wpg, wpu)


def _head_mean_square(x, ones_bd):
    return _dot(_bf16(x * x), ones_bd) * (1.0 / MB_HD)


def _norm_rope(x, ms, gain, cos_t, sin_lo, sin_hi):
    xn = x * lax.rsqrt(ms + EPS) * gain
    outs = []
    for s in range(x.shape[1] // LANES):
        xs = xn[:, s * LANES:(s + 1) * LANES]
        outs.append(xs * cos_t + pltpu.roll(xs, LANES - ROPE_HALF, 1) * sin_lo + pltpu.roll(xs, ROPE_HALF, 1) * sin_hi)
    return jnp.concatenate(outs, axis=1)


def _proj_kernel(h_ref, gq_ref, gkv_ref, wq_ref, wk_ref, wv_ref, qn_ref, kn_ref, bd_ref, cos_ref, slo_ref, shi_ref,
                 q_ref, k_ref, vt_ref, km_ref):
    r = _rms_rows(h_ref[...])
    hq = _bf16(r * gq_ref[...])
    hkv = _bf16(r * gkv_ref[...])
    tables = (cos_ref[...], slo_ref[...], shi_ref[...])
    bd = bd_ref[...]
    kvw = MB_KV_HEADS * MB_HD
    slab = lambda s: slice(s * kvw, (s + 1) * kvw)
    nq = D_MODEL // kvw
    raw = [_dot(hq, wq_ref[:, slab(s)]) for s in range(nq)] + [_dot(hkv, wk_ref[:, slab(s)]) for s in range(2)]
    gains = [qn_ref[:, slab(s)] for s in range(nq)] + [kn_ref[:, slab(s)] for s in range(2)]
    vs = _dot(hkv, wv_ref[...])
    means = [_head_mean_square(x, bd) for x in raw]
    vt_ref[0] = _bf16(vs.T)
    for s in range(nq):
        q_ref[:, slab(s)] = _bf16(_norm_rope(raw[s], means[s], gains[s], *tables))
    for s in range(2):
        ks = _norm_rope(raw[nq + s], means[nq + s], gains[nq + s], *tables)
        k_ref[:, slab(s)] = _bf16(ks)
        for b in range(ks.shape[0] // MB_BLOCK):
            km_ref[b, :, slab(s)] = jnp.mean(ks[b * MB_BLOCK:(b + 1) * MB_BLOCK], axis=0, keepdims=True)


def _proj(h, gq, gkv, wq, wk2, wv, qn, kn2, bd, cos_t, sin_lo, sin_hi, seq):
    t = h.shape[0]
    tm = ROW_TILE
    kvw = MB_KV_HEADS * MB_HD
    row = lambda n: pl.BlockSpec((tm, n), lambda i: (i, 0))
    tab = pl.BlockSpec((tm, LANES), lambda i: (i % (seq // tm), 0))
    return pl.pallas_call(
        _proj_kernel,
        grid=(t // tm,),
        in_specs=[row(D_MODEL), _const_spec((1, D_MODEL)), _const_spec((1, D_MODEL)), _const_spec(wq.shape),
                  _const_spec(wk2.shape), _const_spec(wv.shape), _const_spec((1, D_MODEL)), _const_spec((1, 2 * kvw)),
                  _const_spec(bd.shape), tab, tab, tab],
        out_specs=[row(D_MODEL), row(2 * kvw),
                   pl.BlockSpec((1, kvw, tm), lambda i: (i // (seq // tm), 0, i % (seq // tm))),
                   pl.BlockSpec((tm // MB_BLOCK, 1, 2 * kvw), lambda i: (i, 0, 0))],
        out_shape=[jax.ShapeDtypeStruct((t, D_MODEL), jnp.bfloat16),
                   jax.ShapeDtypeStruct((t, 2 * kvw), jnp.bfloat16),
                   jax.ShapeDtypeStruct((t // seq, kvw, seq), jnp.bfloat16),
                   jax.ShapeDtypeStruct((t // MB_BLOCK, 1, 2 * kvw), jnp.float32)],
        compiler_params=_params(("parallel",)),
        name="proj",
    )(h, gq, gkv, wq, wk2, wv, qn, kn2, bd, cos_t, sin_lo, sin_hi)


MB_GROUPS_PER_STEP = 4


def _moba_kernel(*refs, i):
    q_ref, k_ref, vt_ref, km_ref = refs[:4]
    o_ref, qa_ref, s_ref, acc_ref, vaug_ref = refs[-5:]
    nb = km_ref.shape[1]
    gw = MB_GROUP * MB_HD
    streams = [(gg, h) for gg in range(MB_GROUPS_PER_STEP) for h in range(MB_GROUP)]
    sid = lambda gg, h: gg * MB_GROUP + h
    lanes = lambda gg: slice(gg * LANES, (gg + 1) * LANES)
    nkeys = (i + 1) * MB_BLOCK
    depth = s_ref.shape[1]
    first_half = lax.broadcasted_iota(jnp.int32, (1, LANES), 1) < MB_HD
    zero = jnp.zeros((), jnp.bfloat16)
    for gg in range(MB_GROUPS_PER_STEP):
        for u in range(MB_GROUP // 2):
            t = q_ref[:, gg * gw + u * LANES:gg * gw + (u + 1) * LANES]
            qa_ref[sid(gg, 2 * u)] = jnp.where(first_half, t, zero)
            qa_ref[sid(gg, 2 * u + 1)] = jnp.where(first_half, zero, t)
        vaug_ref[gg, :MB_HD, :] = vt_ref[0, gg * MB_HD:(gg + 1) * MB_HD, :]
        vaug_ref[gg, MB_HD:, :] = jnp.ones((vaug_ref.shape[1] - MB_HD, nkeys), jnp.bfloat16)

    blk = lax.broadcasted_iota(jnp.int32, (nb, MB_BLOCK), 0)
    past = blk < i
    key_i = lax.broadcasted_iota(jnp.int32, (MB_BLOCK, MB_BLOCK), 0)
    qry_i = lax.broadcasted_iota(jnp.int32, (MB_BLOCK, MB_BLOCK), 1)
    causal = key_i <= qry_i

    def scores_of(n, gg):
        maxes = {}
        for h in range(MB_GROUP):
            s = _dot_nt(k_ref[0, n, :, lanes(gg)], qa_ref[sid(gg, h)])
            s_ref[sid(gg, h), n % depth] = s
            maxes[gg, h] = jnp.max(s, axis=0, keepdims=True)
        return maxes

    def weights(s, offset):
        return jnp.exp2(_bf16(s - offset))

    own, chosen = {}, {}
    for gg in range(MB_GROUPS_PER_STEP):
        km = km_ref[0, :, 0, lanes(gg)]
        km_hi = _bf16(km)
        km_lo = _bf16(km - km_hi.astype(jnp.float32))
        own_and_means = jnp.concatenate([k_ref[0, i, :, lanes(gg)], km_hi, km_lo], axis=0)
        for h in range(MB_GROUP):
            r = _dot_nt(own_and_means, qa_ref[sid(gg, h)])
            own[gg, h] = jnp.where(causal, r[:MB_BLOCK], NEG_BIG)
            g = jnp.where(past, r[MB_BLOCK:MB_BLOCK + nb] + r[MB_BLOCK + nb:], -jnp.inf)
            picked = jnp.zeros((nb, MB_BLOCK), jnp.bool_)
            for _ in range(MB_TOPK):
                top = jnp.max(g, axis=0, keepdims=True)
                first = jnp.min(jnp.where(g == top, blk, nb), axis=0, keepdims=True)
                pick = blk == first
                picked = picked | pick
                g = jnp.where(pick, -jnp.inf, g)
            chosen[gg, h] = picked & past
    block_max = {}
    if i > 0:
        for gg in range(MB_GROUPS_PER_STEP):
            block_max.update(scores_of(0, gg))

    ms, ls = {}, {}
    acc_rows = lambda gg, h: slice(sid(gg, h) * MB_HD, (sid(gg, h) + 1) * MB_HD)
    for gg, h in streams:
        mx = jnp.max(own[gg, h], axis=0, keepdims=True)
        pv = _dot(vaug_ref[gg, :, i * MB_BLOCK:nkeys], weights(own[gg, h], mx))
        ms[gg, h] = mx
        ls[gg, h] = pv[MB_HD:MB_HD + 1, :]
        acc_ref[acc_rows(gg, h), :] = pv[:MB_HD, :]

    for n in range(i):
        next_max = {}
        for gg in range(MB_GROUPS_PER_STEP):
            if n + 1 < i:
                next_max.update(scores_of(n + 1, gg))
            for h in range(MB_GROUP):
                use = chosen[gg, h][n:n + 1]
                m_new = jnp.maximum(ms[gg, h], jnp.where(use, block_max[gg, h], NEG_BIG))
                alpha = jnp.exp2(ms[gg, h] - m_new)
                p = weights(s_ref[sid(gg, h), n % depth], jnp.where(use, m_new, -NEG_BIG))
                pv = _dot(vaug_ref[gg, :, n * MB_BLOCK:(n + 1) * MB_BLOCK], p)
                ms[gg, h] = m_new
                ls[gg, h] = alpha * ls[gg, h] + pv[MB_HD:MB_HD + 1, :]
                acc_ref[acc_rows(gg, h), :] = alpha * acc_ref[acc_rows(gg, h), :] + pv[:MB_HD, :]
        block_max = next_max

    for gg, h in streams:
        acc_ref[acc_rows(gg, h), :] = acc_ref[acc_rows(gg, h), :] / ls[gg, h]
    for gg in range(MB_GROUPS_PER_STEP):
        o_ref[:, gg * gw:(gg + 1) * gw] = _bf16(acc_ref[gg * gw:(gg + 1) * gw, :].T)


def _moba(q, k, vt, km, batch, seq):
    nb = seq // MB_BLOCK
    t = batch * seq
    gs = MB_GROUPS_PER_STEP
    gw = gs * MB_GROUP * MB_HD
    n_streams = gs * MB_GROUP
    y = q
    for i in range(nb):
        nkeys = (i + 1) * MB_BLOCK
        y = pl.pallas_call(
            functools.partial(_moba_kernel, i=i),
            grid=(batch, MB_KV_HEADS // gs),
            in_specs=[pl.BlockSpec((MB_BLOCK, gw), lambda b, g, i=i: (b * nb + i, g)),
                      pl.BlockSpec((1, i + 1, MB_BLOCK, gs * LANES), lambda b, g: (b, 0, 0, g)),
                      pl.BlockSpec((1, gs * MB_HD, nkeys), lambda b, g: (b, g, 0)),
                      pl.BlockSpec((1, nb, 1, gs * LANES), lambda b, g: (b, 0, 0, g))],
            out_specs=pl.BlockSpec((MB_BLOCK, gw), lambda b, g, i=i: (b * nb + i, g)),
            out_shape=jax.ShapeDtypeStruct((t, MB_HEADS * MB_HD), jnp.bfloat16),
            input_output_aliases={0: 0},
            scratch_shapes=[pltpu.VMEM((n_streams, MB_BLOCK, LANES), jnp.bfloat16),
                            pltpu.VMEM((n_streams, min(max(i, 1), 3), MB_BLOCK, MB_BLOCK), jnp.float32),
                            pltpu.VMEM((n_streams * MB_HD, MB_BLOCK), jnp.float32),
                            pltpu.VMEM((gs, MB_HD + 16, nkeys), jnp.bfloat16)],
            compiler_params=_params(("parallel", "parallel")),
            name=f"moba{i}",
        )(y, k, vt, km)
    return y


def _rope_tables(seq):
    inv = ROPE_THETA ** (-np.arange(0, ROPE_DIM, 2, dtype=np.float64) / ROPE_DIM)
    ang = np.arange(seq, dtype=np.float64)[:, None] * inv[None, :]
    cos, sin = np.cos(ang), np.sin(ang)
    zeros = np.zeros((seq, MB_HD - ROPE_DIM))
    zh = np.zeros((seq, ROPE_HALF))
    cos_h = np.concatenate([cos, cos, np.ones_like(zeros)], axis=1)
    lo_h = np.concatenate([-sin, zh, zeros], axis=1)
    hi_h = np.concatenate([zh, sin, zeros], axis=1)
    two = lambda a: jnp.asarray(np.concatenate([a, a], axis=1), jnp.float32)
    return two(cos_h), two(lo_h), two(hi_h)


def kernel(x, p, norm_mix, a_w_in, a_b_gate, a_mh_gain, a_w_out, kv_norm, w_kv, k_norm, b_w_q, b_q_norm, b_w_o,
           norm_ffn, w_gate_up, w_down, norm_ple, w_ple_gate, w_ple_up):
    batch, seq, d = x.shape
    t = batch * seq
    f32 = jnp.float32
    x2 = x.reshape(t, d)
    p2 = p.reshape(p.shape[0], t, PLE_DIM)
    row = lambda a: a.reshape(1, -1).astype(f32)

    w_in = a_w_in[0]
    n_main = 2 * ML_QK_COLS + 2 * ML_V_COLS
    w_main = _bf16(w_in[:, :n_main])
    w_gate = _bf16(jnp.pad(w_in[:, n_main:], ((0, 0), (0, GATE_LANES - 2 * ML_HEADS))))
    b_gate = jnp.pad(a_b_gate[0].astype(f32), (0, GATE_LANES - 2 * ML_HEADS)).reshape(1, GATE_LANES)
    q, k, v, o, gcol, grow = _inproj(x2, row(norm_mix[0]), w_main, w_gate, b_gate, seq)
    y = _mlstm(q, k, v, o, gcol, grow, row(a_mh_gain[0]), batch, seq)
    h = _post(x2, y, p2, 0, _bf16(a_w_out[0]), row(norm_ffn[0]), _bf16(w_gate_up[0]), _bf16(w_down[0]),
              row(norm_ple[0]), _bf16(w_ple_gate[0]), _bf16(w_ple_up[0]))

    nb = seq // MB_BLOCK
    kvw = MB_KV_HEADS * MB_HD
    cos_t, sin_lo, sin_hi = _rope_tables(seq)
    head = lax.broadcasted_iota(jnp.int32, (kvw, kvw), 0) // MB_HD
    ones_bd = (head == head.T).astype(jnp.bfloat16)
    qn = row(jnp.tile(b_q_norm[0], MB_HEADS)) * MB_Q_SCALE
    kn2 = row(jnp.tile(k_norm, 2 * MB_KV_HEADS))
    w_k = _bf16(w_kv[:, :kvw]).reshape(d, MB_KV_HEADS, 1, MB_HD)
    w_k2 = jnp.broadcast_to(w_k, (d, MB_KV_HEADS, 2, MB_HD)).reshape(d, 2 * kvw)
    q, k2, vt, km2 = _proj(h, row(norm_mix[1]), row(kv_norm), _bf16(b_w_q[0]), w_k2, _bf16(w_kv[:, kvw:]), qn, kn2,
                           ones_bd, cos_t, sin_lo, sin_hi, seq)
    y = _moba(q, k2.reshape(batch, nb, MB_BLOCK, 2 * kvw), vt, km2.reshape(batch, nb, 1, 2 * kvw), batch, seq)
    h = _post(h, y, p2, 1, _bf16(b_w_o[0]), row(norm_ffn[1]), _bf16(w_gate_up[1]), _bf16(w_down[1]),
              row(norm_ple[1]), _bf16(w_ple_gate[1]), _bf16(w_ple_up[1]))
    return h.reshape(batch, seq, d)
```
